```python
import math
import jax, jax.numpy as jnp
from jax import lax
import numpy as np

D_MODEL = 2048
BATCH = 2
SEQ = 4096
DEPTH = 4
DEC_BATCH = 128
DEC_SEQ = 4
PAST_LEN = 8192
PAGE_SIZE = 128

MLA_HEADS = 8
MLA_NOPE = 128
MLA_ROPE = 64
MLA_V = 128
MLA_Q_LORA = 512
MLA_KV_LORA = 256
ROPE_THETA = 10000.0
ATTN_BLOCK = 128
SB_HEADS = 4
SB_HEAD_DIM = 128
HG_HEADS = 4
HG_KEY = 128
HG_VAL = 128
HG_CHUNK = 64
D_FF = -(-8 * D_MODEL // (3 * 256)) * 256
RMS_EPS = 1e-6

IN_SIZES = (MLA_Q_LORA, MLA_KV_LORA, MLA_ROPE,
            SB_HEADS * SB_HEAD_DIM, SB_HEAD_DIM, SB_HEAD_DIM,
            HG_HEADS * HG_KEY, HG_HEADS * HG_KEY, HG_HEADS * HG_VAL, HG_HEADS * HG_VAL,
            D_MODEL, D_MODEL, D_MODEL)
IN_TOTAL = sum(IN_SIZES)

kernel_name = 'hybrid_mla_stickbreak_hgrn2_step'

F32 = jnp.float32


def _rmsnorm(x, g):
    xf = x.astype(F32)
    y = xf * lax.rsqrt(jnp.mean(xf * xf, axis=-1, keepdims=True) + RMS_EPS)
    return (y * g.astype(F32)).astype(x.dtype)


def _rope(x, pos):
    half = MLA_ROPE // 2
    inv = ROPE_THETA ** (-jnp.arange(half, dtype=F32) / half)
    ang = pos.astype(F32)[:, None] * inv[None, :]
    ang = ang.reshape((ang.shape[0],) + (1,) * (x.ndim - 3) + (half,))
    cos, sin = jnp.cos(ang), jnp.sin(ang)
    xf = x.astype(F32)
    x1, x2 = xf[..., :half], xf[..., half:]
    return jnp.concatenate([x1 * cos - x2 * sin, x2 * cos + x1 * sin], axis=-1).astype(x.dtype)


def _prep(h, pos, lb, w_in, q_norm, w_uq, kv_norm):
    B, T, _ = h.shape
    (c_q, c_kv, k_pe, sb_q, sb_k, sb_v, hg_q, hg_f, hg_i, hg_g,
     g_mla, g_sb, g_hg) = jnp.split(h @ w_in, np.cumsum(IN_SIZES)[:-1].tolist(), axis=-1)
    q = (_rmsnorm(c_q, q_norm) @ w_uq).reshape(B, T, MLA_HEADS, MLA_NOPE + MLA_ROPE)
    q_nope = q[..., :MLA_NOPE]
    q_rope = _rope(q[..., MLA_NOPE:], pos)
    ckv = _rmsnorm(c_kv, kv_norm)
    kr = _rope(k_pe, pos)
    lbf = lb.astype(F32)
    logf = jnp.logaddexp(jnp.log(lbf), jnp.log1p(-lbf) + jax.nn.log_sigmoid(hg_f.astype(F32)))
    return (q_nope, q_rope, ckv, kr,
            sb_q.reshape(B, T, SB_HEADS, SB_HEAD_DIM), sb_k, sb_v,
            hg_q.reshape(B, T, HG_HEADS, HG_KEY), logf.reshape(B, T, HG_HEADS, HG_KEY),
            hg_i.reshape(B, T, HG_HEADS, HG_VAL), hg_g, (g_mla, g_sb, g_hg))


def _mla_prompt(q_nope, q_rope, ckv, kr, w_uk, w_uv):
    B, S, H, _ = q_nope.shape
    scale = (MLA_NOPE + MLA_ROPE) ** -0.5
    k_nope = jnp.einsum('bsc,chn->bshn', ckv, w_uk)
    v = jnp.einsum('bsc,chv->bshv', ckv, w_uv)
    nb = S // ATTN_BLOCK
    qn = q_nope.reshape(B, nb, ATTN_BLOCK, H, MLA_NOPE).transpose(1, 0, 2, 3, 4)
    qr = q_rope.reshape(B, nb, ATTN_BLOCK, H, MLA_ROPE).transpose(1, 0, 2, 3, 4)
    starts = jnp.arange(nb, dtype=jnp.int32) * ATTN_BLOCK
    kpos = jnp.arange(S, dtype=jnp.int32)

    def blk(args):
        qn_b, qr_b, st = args
        s = (jnp.einsum('bthn,bshn->bhts', qn_b, k_nope)
             + jnp.einsum('bthr,bsr->bhts', qr_b, kr)).astype(F32) * scale
        tpos = st + jnp.arange(ATTN_BLOCK, dtype=jnp.int32)
        s = jnp.where(kpos[None, :] <= tpos[:, None], s, -jnp.inf)
        p = jax.nn.softmax(s, axis=-1).astype(v.dtype)
        return jnp.einsum('bhts,bshv->bthv', p, v)

    out = lax.map(blk, (qn, qr, starts))
    return out.transpose(1, 0, 2, 3, 4).reshape(B, S, H * MLA_V)


def _mla_sample(q_nope, q_rope, ckv, kr, pool_ckv, pool_kr, page_table, l, w_uk, w_uv):
    DB, T, H, _ = q_nope.shape
    scale = (MLA_NOPE + MLA_ROPE) ** -0.5
    c_past = pool_ckv[l, page_table].reshape(DB, -1, MLA_KV_LORA)
    r_past = pool_kr[l, page_table].reshape(DB, -1, MLA_ROPE)
    n_past = c_past.shape[1]
    q_lat = jnp.einsum('bthn,chn->bthc', q_nope, w_uk)
    s_past = jnp.einsum('bthc,bsc->bhts', q_lat, c_past) + jnp.einsum('bthr,bsr->bhts', q_rope, r_past)
    s_new = jnp.einsum('bthc,bsc->bhts', q_lat, ckv) + jnp.einsum('bthr,bsr->bhts', q_rope, kr)
    ar = jnp.arange(T)
    s_new = jnp.where(ar[None, :] <= ar[:, None], s_new.astype(F32), -jnp.inf)
    s = jnp.concatenate([s_past.astype(F32), s_new], axis=-1) * scale
    p = jax.nn.softmax(s, axis=-1).astype(ckv.dtype)
    o_lat = (jnp.einsum('bhts,bsc->bthc', p[..., :n_past], c_past)
             + jnp.einsum('bhts,bsc->bthc', p[..., n_past:], ckv))
    return jnp.einsum('bthc,chv->bthv', o_lat, w_uv).reshape(DB, T, H * MLA_V)


def _sb_weights(z, valid, carry):
    neg = jnp.where(valid, jax.nn.log_sigmoid(-z), 0.0)
    after = lax.cumsum(neg, axis=z.ndim - 1, reverse=True) - neg + carry[..., None]
    a = jnp.where(valid, jnp.exp(jax.nn.log_sigmoid(z) + after), 0.0)
    return a, carry + jnp.sum(neg, axis=-1)


def _sb_prompt(q, k, v):
    B, S, H, D = q.shape
    scale = D ** -0.5
    nb = S // ATTN_BLOCK
    qb = q.reshape(B, nb, ATTN_BLOCK, H, D).transpose(1, 0, 3, 2, 4)
    starts = jnp.arange(nb, dtype=jnp.int32) * ATTN_BLOCK
    kpos = jnp.arange(S, dtype=jnp.int32)

    def blk(args):
        q_b, st = args
        z = jnp.einsum('bhtd,bsd->bhts', q_b, k).astype(F32) * scale
        valid = kpos[None, :] < (st + jnp.arange(ATTN_BLOCK, dtype=jnp.int32))[:, None]
        a, _ = _sb_weights(z, valid, jnp.zeros(z.shape[:-1], F32))
        return jnp.einsum('bhts,bsd->bthd', a.astype(v.dtype), v)

    out = lax.map(blk, (qb, starts))
    return out.transpose(1, 0, 2, 3, 4).reshape(B, S, H * D)


def _sb_sample(q, k, v, pool_k, pool_v, page_table, l):
    DB, T, H, D = q.shape
    scale = D ** -0.5
    qh = q.transpose(0, 2, 1, 3)
    ar = jnp.arange(T)
    z = jnp.einsum('bhtd,bsd->bhts', qh, k).astype(F32) * scale
    a, carry = _sb_weights(z, ar[None, :] < ar[:, None], jnp.zeros((DB, H, T), F32))
    acc = jnp.einsum('bhts,bsd->bhtd', a.astype(v.dtype), v).astype(F32)
    all_valid = jnp.ones((T, PAGE_SIZE), dtype=bool)

    def step(c, pt):
        acc, carry = c
        kp = pool_k[l, pt]
        vp = pool_v[l, pt]
        zp = jnp.einsum('bhtd,bsd->bhts', qh, kp).astype(F32) * scale
        ap, carry = _sb_weights(zp, all_valid, carry)
        acc = acc + jnp.einsum('bhts,bsd->bhtd', ap.astype(vp.dtype), vp).astype(F32)
        return (acc, carry), None

    (acc, _), _ = lax.scan(step, (acc, carry), page_table.T[::-1])
    return acc.astype(q.dtype).transpose(0, 2, 1, 3).reshape(DB, T, H * D)


def _hgrn2(q, logf, i, s0):
    B, T, H, K = q.shape
    V = i.shape[-1]
    C = math.gcd(T, HG_CHUNK)
    n = T // C
    qf = q.astype(F32) * (K ** -0.5)
    kf = -jnp.expm1(logf)

    def chunks(a):
        return a.reshape(B, n, C, H, a.shape[-1]).transpose(1, 0, 3, 2, 4)

    tri = (jnp.arange(C)[None, :] <= jnp.arange(C)[:, None])[..., None]

    def step(S, xs):
        qc, gc, kc, ic = xs
        G = jnp.cumsum(gc, axis=-2)
        o = jnp.einsum('bhtk,bhkv->bhtv', qc * jnp.exp(G), S)
        Dg = G[..., :, None, :] - G[..., None, :, :]
        dec = jnp.where(tri, jnp.exp(jnp.where(tri, Dg, 0.0)), 0.0)
        A = jnp.einsum('bhtk,bhsk,bhtsk->bhts', qc, kc, dec)
        o = o + jnp.einsum('bhts,bhsv->bhtv', A, ic)
        Gl = G[..., -1:, :]
        S = jnp.exp(Gl[..., 0, :])[..., None] * S + jnp.einsum('bhsk,bhsv->bhkv', kc * jnp.exp(Gl - G), ic)
        return S, o

    S, o = lax.scan(step, s0.astype(F32), (chunks(qf), chunks(logf), chunks(kf), chunks(i.astype(F32))))
    return o.transpose(1, 0, 3, 2, 4).reshape(B, T, H, V), S


def _merge(mla_o, sb_o, hg_o, hg_g, gates, hg_norm_l, lift_mla, lift_sb, lift_hg, w_out_l):
    B, T = hg_g.shape[:2]
    dt = hg_g.dtype
    hg = (_rmsnorm(hg_o, hg_norm_l).reshape(B, T, -1) * jax.nn.silu(hg_g.astype(F32))).astype(dt)
    g_mla, g_sb, g_hg = gates
    m = (jax.nn.sigmoid(g_mla.astype(F32)) * (mla_o @ lift_mla).astype(F32)
         + jax.nn.sigmoid(g_sb.astype(F32)) * (sb_o @ lift_sb).astype(F32)
         + jax.nn.sigmoid(g_hg.astype(F32)) * (hg @ lift_hg).astype(F32))
    return m.astype(dt) @ w_out_l


def _ffn(h, wg, wu, wd):
    return (jax.nn.silu(h @ wg) * (h @ wu)) @ wd


def setup_inputs(seed: int = 0) -> dict:
    key = jax.random.key(seed)
    ks = iter(jax.random.split(key, 40))
    n_pages = PAST_LEN // PAGE_SIZE
    n_pool = (DEC_BATCH * n_pages * 5) // 4

    def nrm(shape, scale=1.0):
        return jax.random.normal(next(ks), shape, F32) * scale

    def gain(shape):
        return 1.0 + nrm(shape, 0.02)

    x_prompt = nrm((BATCH, SEQ, D_MODEL))
    x_sample = nrm((DEC_BATCH, DEC_SEQ, D_MODEL))
    cache_mla_ckv = nrm((DEPTH, n_pool, PAGE_SIZE, MLA_KV_LORA))
    cache_mla_krope = nrm((DEPTH, n_pool, PAGE_SIZE, MLA_ROPE))
    cache_sb_k = nrm((DEPTH, n_pool, PAGE_SIZE, SB_HEAD_DIM))
    cache_sb_v = nrm((DEPTH, n_pool, PAGE_SIZE, SB_HEAD_DIM))
    state_hgrn = nrm((DEPTH, DEC_BATCH, HG_HEADS, HG_KEY, HG_VAL), 0.5)
    page_table = jax.random.permutation(next(ks), n_pool)[: DEC_BATCH * n_pages].reshape(
        DEC_BATCH, n_pages).astype(jnp.int32)
    return {
        'x_prompt': x_prompt,
        'x_sample': x_sample,
        'cache_mla_ckv': cache_mla_ckv,
        'cache_mla_krope': cache_mla_krope,
        'cache_sb_k': cache_sb_k,
        'cache_sb_v': cache_sb_v,
        'state_hgrn': state_hgrn,
        'page_table': page_table,
        'norm_attn': gain((DEPTH, D_MODEL)),
        'w_in': nrm((DEPTH, D_MODEL, IN_TOTAL), D_MODEL ** -0.5),
        'mla_q_norm': gain((DEPTH, MLA_Q_LORA)),
        'mla_w_uq': nrm((DEPTH, MLA_Q_LORA, MLA_HEADS * (MLA_NOPE + MLA_ROPE)), MLA_Q_LORA ** -0.5),
        'mla_kv_norm': gain((DEPTH, MLA_KV_LORA)),
        'mla_w_uk': nrm((DEPTH, MLA_KV_LORA, MLA_HEADS, MLA_NOPE), MLA_KV_LORA ** -0.5),
        'mla_w_uv': nrm((DEPTH, MLA_KV_LORA, MLA_HEADS, MLA_V), MLA_KV_LORA ** -0.5),
        'hg_lb_logits': nrm((DEPTH, HG_HEADS * HG_KEY), 0.5),
        'hg_norm': gain((DEPTH, HG_VAL)),
        'w_lift_mla': nrm((DEPTH, MLA_HEADS * MLA_V, D_MODEL), (MLA_HEADS * MLA_V) ** -0.5),
        'w_lift_sb': nrm((DEPTH, SB_HEADS * SB_HEAD_DIM, D_MODEL), (SB_HEADS * SB_HEAD_DIM) ** -0.5),
        'w_lift_hg': nrm((DEPTH, HG_HEADS * HG_VAL, D_MODEL), (HG_HEADS * HG_VAL) ** -0.5),
        'w_out': nrm((DEPTH, D_MODEL, D_MODEL), D_MODEL ** -0.5),
        'norm_ffn': gain((DEPTH, D_MODEL)),
        'ffn_w_gate': nrm((DEPTH, D_MODEL, D_FF), D_MODEL ** -0.5),
        'ffn_w_up': nrm((DEPTH, D_MODEL, D_FF), D_MODEL ** -0.5),
        'ffn_w_down': nrm((DEPTH, D_FF, D_MODEL), D_FF ** -0.5),
        'norm_final': gain((D_MODEL,)),
    }


def reference(x_prompt, x_sample, cache_mla_ckv, cache_mla_krope, cache_sb_k, cache_sb_v, state_hgrn,
              page_table, norm_attn, w_in, mla_q_norm, mla_w_uq, mla_kv_norm, mla_w_uk, mla_w_uv,
              hg_lb_logits, hg_norm, w_lift_mla, w_lift_sb, w_lift_hg, w_out, norm_ffn,
              ffn_w_gate, ffn_w_up, ffn_w_down, norm_final):
    dt = x_prompt.dtype
    past_len = page_table.shape[1] * PAGE_SIZE
    pos_p = jnp.arange(x_prompt.shape[1], dtype=jnp.int32)
    pos_s = past_len + jnp.arange(x_sample.shape[1], dtype=jnp.int32)
    cum = jnp.cumsum(jax.nn.softmax(hg_lb_logits.astype(F32), axis=0), axis=0)
    lower_bounds = cum - cum[:1]

    xp, xs = x_prompt, x_sample
    p_ckv, p_kr, p_sk, p_sv, p_hs = [], [], [], [], []
    s_ckv, s_kr, s_sk, s_sv, s_hs = [], [], [], [], []
    for l in range(DEPTH):
        (qn, qr, ckv, kr, sq, sk, sv, hq, hlogf, hi, hg, gates) = _prep(
            _rmsnorm(xp, norm_attn[l]), pos_p, lower_bounds[l], w_in[l], mla_q_norm[l], mla_w_uq[l],
            mla_kv_norm[l])
        mla_o = _mla_prompt(qn, qr, ckv, kr, mla_w_uk[l], mla_w_uv[l])
        sb_o = _sb_prompt(sq, sk, sv)
        hs0 = jnp.zeros((xp.shape[0], HG_HEADS, HG_KEY, HG_VAL), F32)
        hg_o, hs = _hgrn2(hq, hlogf, hi, hs0)
        xp = xp + _merge(mla_o, sb_o, hg_o, hg, gates, hg_norm[l], w_lift_mla[l], w_lift_sb[l],
                         w_lift_hg[l], w_out[l])
        xp = xp + _ffn(_rmsnorm(xp, norm_ffn[l]), ffn_w_gate[l], ffn_w_up[l], ffn_w_down[l])
        p_ckv.append(ckv)
        p_kr.append(kr)
        p_sk.append(sk)
        p_sv.append(sv)
        p_hs.append(hs)
        (qn, qr, ckv, kr, sq, sk, sv, hq, hlogf, hi, hg, gates) = _prep(
            _rmsnorm(xs, norm_attn[l]), pos_s, lower_bounds[l], w_in[l], mla_q_norm[l], mla_w_uq[l],
            mla_kv_norm[l])
        mla_o = _mla_sample(qn, qr, ckv, kr, cache_mla_ckv, cache_mla_krope, page_table, l,
                            mla_w_uk[l], mla_w_uv[l])
        sb_o = _sb_sample(sq, sk, sv, cache_sb_k, cache_sb_v, page_table, l)
        hg_o, hs = _hgrn2(hq, hlogf, hi, state_hgrn[l])
        xs = xs + _merge(mla_o, sb_o, hg_o, hg, gates, hg_norm[l], w_lift_mla[l], w_lift_sb[l],
                         w_lift_hg[l], w_out[l])
        xs = xs + _ffn(_rmsnorm(xs, norm_ffn[l]), ffn_w_gate[l], ffn_w_up[l], ffn_w_down[l])
        s_ckv.append(ckv)
        s_kr.append(kr)
        s_sk.append(sk)
        s_sv.append(sv)
        s_hs.append(hs)

    y_prompt = _rmsnorm(xp, norm_final)
    y_sample = _rmsnorm(xs, norm_final)
    return (y_prompt, y_sample,
            jnp.stack(p_ckv).astype(dt), jnp.stack(p_kr).astype(dt), jnp.stack(p_sk).astype(dt),
            jnp.stack(p_sv).astype(dt), jnp.stack(p_hs).astype(dt),
            jnp.stack(s_ckv).astype(dt), jnp.stack(s_kr).astype(dt), jnp.stack(s_sk).astype(dt),
            jnp.stack(s_sv).astype(dt), jnp.stack(s_hs).astype(dt))
```

```python
import functools

import jax
import jax.numpy as jnp
from jax import lax
from jax.experimental import pallas as pl
from jax.experimental.pallas import tpu as pltpu

F32 = jnp.float32
BF16 = jnp.bfloat16

D_MODEL = 2048
DEPTH = 4
N_BATCH = 2
SEQ = 4096
DEC_BATCH = 128
DEC_SEQ = 4
PAGE = 128
N_PAGES = 64
P_ROWS = N_BATCH * SEQ
S_ROWS = DEC_BATCH * DEC_SEQ
M_ROWS = P_ROWS + S_ROWS

MLA_HEADS = 8
MLA_NOPE = 128
MLA_ROPE = 64
MLA_V = 128
MLA_Q_LORA = 512
MLA_KV_LORA = 256
MLA_QK = 256
ROPE_THETA = 10000.0
SB_HEADS = 4
SB_DIM = 128
HG_HEADS = 4
HG_DIM = 128
HG_CHUNK = 64
D_FF = 5632
RMS_EPS = 1e-6

C_CQ, C_SBQ, C_CKV, C_SBK, C_SBV = 0, 512, 1024, 1280, 1408
C_HGQ, C_HGF, C_HGI, C_HGG = 1536, 2048, 2560, 3072
C_GMLA, C_GSB, C_GHG, C_KPE = 3584, 5632, 7680, 9728
N_IN = 9984

BM = 1088
VMEM_LIMIT = 48 * 1024 * 1024
VMEM_LIMIT_BIG = 56 * 1024 * 1024

MLA_SCALE = (MLA_NOPE + MLA_ROPE) ** -0.5
SB_SCALE = SB_DIM ** -0.5
HG_SCALE = HG_DIM ** -0.5


def _cp(sem, vmem=VMEM_LIMIT):
    return pltpu.CompilerParams(dimension_semantics=sem, vmem_limit_bytes=vmem)


def _dot(a, b):
    return jnp.dot(a, b, preferred_element_type=F32)


def _dot_nt(a, b):
    return lax.dot_general(a, b, (((1,), (1,)), ((), ())), preferred_element_type=F32)


def _dot_tn(a, b):
    return lax.dot_general(a, b, (((0,), (0,)), ((), ())), preferred_element_type=F32)


def _rms(x, g):
    return x * lax.rsqrt(jnp.mean(x * x, axis=-1, keepdims=True) + RMS_EPS) * g


def _softplus_neg_abs(z):
    return jnp.log1p(jnp.exp(-jnp.abs(z)))


def _split2(x):
    hi = x.astype(BF16)
    lo = (x - hi.astype(F32)).astype(BF16)
    return hi, lo


def _split3(x):
    hi = x.astype(BF16)
    r = x - hi.astype(F32)
    mid = r.astype(BF16)
    lo = (r - mid.astype(F32)).astype(BF16)
    return hi, mid, lo


def _norm_kernel(x_ref, g_ref, o_ref):
    o_ref[...] = _rms(x_ref[...], g_ref[...]).astype(o_ref.dtype)


def rmsnorm(x, g, out_dtype, bm=544):
    m, d = x.shape
    return pl.pallas_call(
        _norm_kernel,
        grid=(m // bm,),
        in_specs=[pl.BlockSpec((bm, d), lambda i: (i, 0)), pl.BlockSpec((1, d), lambda i: (0, 0))],
        out_specs=pl.BlockSpec((bm, d), lambda i: (i, 0)),
        out_shape=jax.ShapeDtypeStruct((m, d), out_dtype),
        compiler_params=_cp(("parallel",)),
        name="rmsnorm",
    )(x, g.reshape(1, d))


def _mm_kernel(a_ref, w_ref, o_ref):
    o_ref[...] = _dot(a_ref[...], w_ref[...]).astype(o_ref.dtype)


def _mm_res_kernel(a_ref, w_ref, r_ref, o_ref):
    o_ref[...] = r_ref[...] + _dot(a_ref[...], w_ref[...])


def matmul(a, w, bn, out_dtype, res=None, bm=BM, vmem=VMEM_LIMIT, name="matmul"):
    m, k = a.shape
    n = w.shape[1]
    in_specs = [pl.BlockSpec((bm, k), lambda i, j: (i, 0)), pl.BlockSpec((k, bn), lambda i, j: (0, j))]
    args = [a, w]
    body = _mm_kernel
    if res is not None:
        in_specs.append(pl.BlockSpec((bm, bn), lambda i, j: (i, j)))
        args.append(res)
        body = _mm_res_kernel
    return pl.pallas_call(
        body,
        grid=(m // bm, n // bn),
        in_specs=in_specs,
        out_specs=pl.BlockSpec((bm, bn), lambda i, j: (i, j)),
        out_shape=jax.ShapeDtypeStruct((m, n), out_dtype),
        compiler_params=_cp(("parallel", "arbitrary"), vmem),
        name=name,
    )(*args)


def _lb_kernel(x_ref, o_ref):
    x = x_ref[...]
    e = jnp.exp(x - jnp.max(x, axis=0, keepdims=True))
    p = e / jnp.sum(e, axis=0, keepdims=True)
    cum0 = p[0:1]
    cum = cum0
    o_ref[0:1, :] = cum - cum0
    for l in range(1, DEPTH):
        cum = cum + p[l:l + 1]
        o_ref[l:l + 1, :] = cum - cum0


def lower_bounds(logits):
    return pl.pallas_call(
        _lb_kernel, out_shape=jax.ShapeDtypeStruct(logits.shape, F32), name="hg_lower_bounds")(logits)


def _rope(x, cos, sin, first):
    rot = jnp.where(first, -pltpu.roll(x, 96, 1), pltpu.roll(x, 32, 1))
    return x * cos + rot * sin


def _prep_kernel(cq_ref, ckv_ref, kpe_ref, hf_ref, cos_ref, sin_ref, qn_ref, wuq_ref, kvn_ref, lb_ref,
                 q_ref, ckvo_ref, kro_ref, logf_ref, kf_ref):
    cos = cos_ref[...]
    sin = sin_ref[...]
    first = lax.broadcasted_iota(jnp.int32, cos.shape, 1) < (MLA_ROPE // 2)
    cqn = _rms(cq_ref[...], qn_ref[...]).astype(BF16)
    q = _dot(cqn, wuq_ref[...])
    for h in range(MLA_HEADS):
        a = h * MLA_QK
        q_ref[:, a:a + 128] = q[:, a:a + 128].astype(BF16)
        q_ref[:, a + 128:a + 256] = _rope(q[:, a + 128:a + 256], cos, sin, first).astype(BF16)
    ckvo_ref[...] = _rms(ckv_ref[...], kvn_ref[...])
    kro_ref[...] = _rope(kpe_ref[...], cos, sin, first)
    z = hf_ref[...]
    lb = lb_ref[...]
    t = _softplus_neg_abs(z)
    a_ = jnp.log(lb)
    b_ = jnp.log1p(-lb) + (jnp.minimum(z, 0.0) - t)
    logf_ref[...] = jnp.maximum(a_, b_) + jnp.log1p(jnp.exp(-jnp.abs(a_ - b_)))
    kf_ref[...] = (1.0 - lb) * jnp.exp(jnp.minimum(-z, 0.0) - t)


def prep(y, cos, sin, q_norm, w_uq, kv_norm, lb, bm=512):
    m = y.shape[0]
    row = lambda c: (lambda i: (i, c))
    const = lambda i: (0, 0)
    return pl.pallas_call(
        _prep_kernel,
        grid=(m // bm,),
        in_specs=[
            pl.BlockSpec((bm, 512), row(C_CQ // 512)),
            pl.BlockSpec((bm, 256), row(C_CKV // 256)),
            pl.BlockSpec((bm, 128), row(C_KPE // 128)),
            pl.BlockSpec((bm, 512), row(C_HGF // 512)),
            pl.BlockSpec((bm, 128), row(0)),
            pl.BlockSpec((bm, 128), row(0)),
            pl.BlockSpec((1, 512), const),
            pl.BlockSpec((MLA_Q_LORA, MLA_HEADS * MLA_QK), const),
            pl.BlockSpec((1, 256), const),
            pl.BlockSpec((1, 512), const),
        ],
        out_specs=[
            pl.BlockSpec((bm, MLA_HEADS * MLA_QK), row(0)),
            pl.BlockSpec((bm, 256), row(0)),
            pl.BlockSpec((bm, 128), row(0)),
            pl.BlockSpec((bm, 512), row(0)),
            pl.BlockSpec((bm, 512), row(0)),
        ],
        out_shape=[
            jax.ShapeDtypeStruct((m, MLA_HEADS * MLA_QK), BF16),
            jax.ShapeDtypeStruct((m, 256), F32),
            jax.ShapeDtypeStruct((m, 128), F32),
            jax.ShapeDtypeStruct((m, 512), F32),
            jax.ShapeDtypeStruct((m, 512), F32),
        ],
        compiler_params=_cp(("parallel",)),
        name="prep",
    )(y, y, y, y, cos, sin, q_norm.reshape(1, -1), w_uq, kv_norm.reshape(1, -1), lb.reshape(1, -1))


def _kvup_kernel(ckv_ref, kr_ref, wuk_ref, wuv_ref, k_ref, v_ref):
    c = ckv_ref[...].astype(BF16)
    kn = _dot(c, wuk_ref[...])
    v_ref[...] = _dot(c, wuv_ref[...]).astype(BF16)
    kr = kr_ref[...].astype(BF16)
    for h in range(MLA_HEADS):
        a = h * MLA_QK
        k_ref[:, a:a + 128] = kn[:, h * 128:(h + 1) * 128].astype(BF16)
        k_ref[:, a + 128:a + 256] = kr


def kv_up(ckv, kr, w_uk, w_uv, bm=512):
    const = lambda i: (0, 0)
    row = lambda i: (i, 0)
    return pl.pallas_call(
        _kvup_kernel,
        grid=(P_ROWS // bm,),
        in_specs=[pl.BlockSpec((bm, 256), row), pl.BlockSpec((bm, 128), row),
                  pl.BlockSpec((256, 1024), const), pl.BlockSpec((256, 1024), const)],
        out_specs=[pl.BlockSpec((bm, MLA_HEADS * MLA_QK), row), pl.BlockSpec((bm, 1024), row)],
        out_shape=[jax.ShapeDtypeStruct((P_ROWS, MLA_HEADS * MLA_QK), BF16),
                   jax.ShapeDtypeStruct((P_ROWS, 1024), BF16)],
        compiler_params=_cp(("parallel",)),
        name="kv_up",
    )(ckv, kr, w_uk, w_uv)


def _mla_flash_kernel(q_ref, k_ref, v_ref, o_ref, m_sc, l_sc, acc_sc, *, tq, tk):
    i = pl.program_id(2)
    j = pl.program_id(3)

    @pl.when(j == 0)
    def _():
        m_sc[...] = jnp.full(m_sc.shape, -jnp.inf, F32)
        l_sc[...] = jnp.zeros(l_sc.shape, F32)
        acc_sc[...] = jnp.zeros(acc_sc.shape, F32)

    def step(masked):
        s = _dot_nt(q_ref[...], k_ref[...]) * MLA_SCALE
        if masked:
            row = lax.broadcasted_iota(jnp.int32, s.shape, 0)
            col = lax.broadcasted_iota(jnp.int32, s.shape, 1)
            s = jnp.where(col <= row, s, -jnp.inf)
        m_prev = m_sc[...]
        m_new = jnp.maximum(m_prev, jnp.max(s, axis=-1, keepdims=True))
        alpha = jnp.exp(m_prev - m_new)
        p = jnp.exp(s - m_new)
        l_sc[...] = alpha * l_sc[...] + jnp.sum(p, axis=-1, keepdims=True)
        acc_sc[...] = alpha * acc_sc[...] + _dot(p.astype(BF16), v_ref[...])
        m_sc[...] = m_new

    @pl.when(j < i)
    def _():
        step(False)

    @pl.when(j == i)
    def _():
        step(True)
        o_ref[...] = (acc_sc[...] / l_sc[...]).astype(o_ref.dtype)


def mla_prompt(q_full, k_full, v, tq=512):
    tk = tq
    nq = SEQ // tq
    kern = functools.partial(_mla_flash_kernel, tq=tq, tk=tk)
    return pl.pallas_call(
        kern,
        grid=(N_BATCH, MLA_HEADS, nq, nq),
        in_specs=[
            pl.BlockSpec((tq, MLA_QK), lambda b, h, i, j: (b * nq + i, h)),
            pl.BlockSpec((tk, MLA_QK), lambda b, h, i, j: (b * nq + jnp.minimum(j, i), h)),
            pl.BlockSpec((tk, MLA_V), lambda b, h, i, j: (b * nq + jnp.minimum(j, i), h)),
        ],
        out_specs=pl.BlockSpec((tq, MLA_V), lambda b, h, i, j: (b * nq + i, h)),
        out_shape=jax.ShapeDtypeStruct((P_ROWS, MLA_HEADS * MLA_V), BF16),
        scratch_shapes=[pltpu.VMEM((tq, 1), F32), pltpu.VMEM((tq, 1), F32), pltpu.VMEM((tq, MLA_V), F32)],
        compiler_params=_cp(("parallel", "parallel", "parallel", "arbitrary")),
        name="mla_prompt",
    )(q_full, k_full, v)


def _sb_prompt_kernel(q_ref, k_ref, v_ref, o_ref, carry_sc, acc_sc, *, tq, tk):
    i = pl.program_id(1)
    j = pl.program_id(2)
    ratio = tq // tk
    last = ratio * i + ratio - 1
    jj = last - j

    @pl.when(j == 0)
    def _():
        carry_sc[...] = jnp.zeros(carry_sc.shape, F32)
        acc_sc[...] = jnp.zeros(acc_sc.shape, F32)

    def step(masked):
        k = k_ref[...].astype(BF16)
        v = v_ref[...].astype(BF16)
        ur = lax.broadcasted_iota(jnp.int32, (tk, tk), 0)
        uc = lax.broadcasted_iota(jnp.int32, (tk, tk), 1)
        later = (ur > uc).astype(BF16)
        if masked:
            row = i * tq + lax.broadcasted_iota(jnp.int32, (tq, tk), 0)
            col = jj * tk + lax.broadcasted_iota(jnp.int32, (tq, tk), 1)
            valid = col < row
        for h in range(SB_HEADS):
            q = q_ref[:, h * SB_DIM:(h + 1) * SB_DIM].astype(BF16)
            z = _dot_nt(q, k) * SB_SCALE
            t = _softplus_neg_abs(z)
            ls = jnp.minimum(z, 0.0) - t
            neg = jnp.minimum(-z, 0.0) - t
            if masked:
                neg = jnp.where(valid, neg, 0.0)
            hi, lo = _split2(neg)
            after = _dot(hi, later) + _dot(lo, later) + carry_sc[h]
            a = jnp.exp(ls + after)
            if masked:
                a = jnp.where(valid, a, 0.0)
            acc_sc[h] = acc_sc[h] + _dot(a.astype(BF16), v)
            carry_sc[h] = carry_sc[h] + jnp.sum(neg, axis=-1, keepdims=True)

    @pl.when(jnp.logical_and(j <= last, j >= ratio))
    def _():
        step(False)

    @pl.when(j < ratio)
    def _():
        step(True)

    @pl.when(j == last)
    def _():
        for h in range(SB_HEADS):
            o_ref[:, h * SB_DIM:(h + 1) * SB_DIM] = acc_sc[h].astype(o_ref.dtype)


def sb_prompt(y, tq=512, tk=256):
    nq = SEQ // tq
    nk = SEQ // tk
    ratio = tq // tk
    kern = functools.partial(_sb_prompt_kernel, tq=tq, tk=tk)

    def kv_map(c):
        def f(b, i, j):
            jj = jnp.maximum(ratio * i + ratio - 1 - j, 0)
            return (b * nk + jj, c)
        return f

    return pl.pallas_call(
        kern,
        grid=(N_BATCH, nq, nk),
        in_specs=[
            pl.BlockSpec((tq, SB_HEADS * SB_DIM), lambda b, i, j: (b * nq + i, C_SBQ // 512)),
            pl.BlockSpec((tk, SB_DIM), kv_map(C_SBK // 128)),
            pl.BlockSpec((tk, SB_DIM), kv_map(C_SBV // 128)),
        ],
        out_specs=pl.BlockSpec((tq, SB_HEADS * SB_DIM), lambda b, i, j: (b * nq + i, 0)),
        out_shape=jax.ShapeDtypeStruct((P_ROWS, SB_HEADS * SB_DIM), BF16),
        scratch_shapes=[pltpu.VMEM((SB_HEADS, tq, 1), F32), pltpu.VMEM((SB_HEADS, tq, SB_DIM), F32)],
        compiler_params=_cp(("parallel", "parallel", "arbitrary")),
        name="sb_prompt",
    )(y, y, y)


def _hgrn_prompt_kernel(q_ref, g_ref, kf_ref, i_ref, o_ref, so_ref, st_sc, dg_sc, *, nchunk):
    tstep = pl.program_id(2)
    C = HG_CHUNK
    SUB = 16

    @pl.when(tstep == 0)
    def _():
        st_sc[...] = jnp.zeros(st_sc.shape, F32)

    lr = lax.broadcasted_iota(jnp.int32, (C, C), 0)
    lc = lax.broadcasted_iota(jnp.int32, (C, C), 1)
    incl = (lc <= lr).astype(BF16)
    sr = lax.broadcasted_iota(jnp.int32, (SUB, HG_DIM), 0)

    def chunk(c, carry):
        base = pl.multiple_of(c * C, C)
        r = pl.ds(base, C)
        g = g_ref[r, :]
        qf = q_ref[r, :] * HG_SCALE
        kf = kf_ref[r, :]
        ii = i_ref[r, :]
        iib = ii.astype(BF16)
        hi, mid, lo = _split3(g)
        G = _dot(incl, hi) + _dot(incl, mid) + _dot(incl, lo)
        st = st_sc[...]
        o = _dot_nt((qf * jnp.exp(G)).astype(BF16), st.astype(BF16))

        def offdiag(t0, t1, s0, s1):
            gm = G[s1 - 1:s1, :]
            qt = (qf[t0:t1] * jnp.exp(G[t0:t1] - gm)).astype(BF16)
            kt = (kf[s0:s1] * jnp.exp(gm - G[s0:s1])).astype(BF16)
            a = _dot_nt(qt, kt)
            return _dot(a.astype(BF16), iib[s0:s1])

        off_a = offdiag(32, 64, 0, 32)
        off_b0 = offdiag(16, 32, 0, 16)
        off_b1 = offdiag(48, 64, 32, 48)

        for blk in range(C // SUB):
            b0 = blk * SUB
            gb = G[b0:b0 + SUB]
            kb = kf[b0:b0 + SUB]
            ib = ii[b0:b0 + SUB]
            for t in range(SUB):
                keep = sr <= t
                dec = jnp.exp(jnp.where(keep, G[b0 + t:b0 + t + 1] - gb, 0.0))
                w = jnp.where(keep, qf[b0 + t:b0 + t + 1] * kb * dec, 0.0)
                a_col = jnp.sum(w, axis=-1, keepdims=True)
                dg_sc[b0 + t:b0 + t + 1, :] = jnp.sum(a_col * ib, axis=0, keepdims=True)

        dg = dg_sc[...]
        o_ref[pl.ds(base, 16), :] = o[0:16] + dg[0:16]
        o_ref[pl.ds(base + 16, 16), :] = o[16:32] + dg[16:32] + off_b0
        o_ref[pl.ds(base + 32, 16), :] = o[32:48] + dg[32:48] + off_a[0:16]
        o_ref[pl.ds(base + 48, 16), :] = o[48:64] + dg[48:64] + off_a[16:32] + off_b1

        g_last = G[C - 1:C, :]
        kd = (kf * jnp.exp(g_last - G)).astype(BF16)
        st_sc[...] = jnp.exp(g_last) * st + _dot_tn(iib, kd)
        return carry

    lax.fori_loop(0, nchunk, chunk, 0)

    @pl.when(tstep == pl.num_programs(2) - 1)
    def _():
        so_ref[0] = st_sc[...].T


def hgrn_prompt(y, logf, kf, tb=512):
    nt = SEQ // tb
    kern = functools.partial(_hgrn_prompt_kernel, nchunk=tb // HG_CHUNK)
    col = lambda c0: (lambda b, h, t: (b * nt + t, c0 + h))
    return pl.pallas_call(
        kern,
        grid=(N_BATCH, HG_HEADS, nt),
        in_specs=[
            pl.BlockSpec((tb, HG_DIM), col(C_HGQ // 128)),
            pl.BlockSpec((tb, HG_DIM), col(0)),
            pl.BlockSpec((tb, HG_DIM), col(0)),
            pl.BlockSpec((tb, HG_DIM), col(C_HGI // 128)),
        ],
        out_specs=[
            pl.BlockSpec((tb, HG_DIM), col(0)),
            pl.BlockSpec((1, HG_DIM, HG_DIM), lambda b, h, t: (b * HG_HEADS + h, 0, 0)),
        ],
        out_shape=[
            jax.ShapeDtypeStruct((P_ROWS, HG_HEADS * HG_DIM), F32),
            jax.ShapeDtypeStruct((N_BATCH * HG_HEADS, HG_DIM, HG_DIM), F32),
        ],
        scratch_shapes=[pltpu.VMEM((HG_DIM, HG_DIM), F32), pltpu.VMEM((HG_CHUNK, HG_DIM), F32)],
        compiler_params=_cp(("parallel", "parallel", "arbitrary")),
        name="hgrn_prompt",
    )(y, logf, kf, y)


def _hgrn_sample_kernel(q_ref, g_ref, kf_ref, i_ref, s_ref, o_ref, so_ref, lhs_sc, kd_sc, i_sc, *, dbb):
    T = DEC_SEQ
    for d in range(dbb):
        for h in range(HG_HEADS):
            cs = slice(h * HG_DIM, (h + 1) * HG_DIM)
            rows = [d * T + t for t in range(T)]
            q = [q_ref[r:r + 1, cs] * HG_SCALE for r in rows]
            g = [g_ref[r:r + 1, cs] for r in rows]
            k = [kf_ref[r:r + 1, cs] for r in rows]
            iv = [i_ref[r:r + 1, cs] for r in rows]
            G = [g[0]]
            for t in range(1, T):
                G.append(G[t - 1] + g[t])
            st = s_ref[d, h].T
            lhs_sc[...] = jnp.zeros(lhs_sc.shape, F32)
            kd_sc[...] = jnp.zeros(kd_sc.shape, F32)
            i_sc[...] = jnp.zeros(i_sc.shape, F32)
            for t in range(T):
                lhs_sc[t:t + 1, :] = q[t] * jnp.exp(G[t])
                kd_sc[t:t + 1, :] = k[t] * jnp.exp(G[T - 1] - G[t])
                i_sc[t:t + 1, :] = iv[t]
            o = _dot_nt(lhs_sc[...].astype(BF16), st.astype(BF16))
            for t in range(T):
                ot = o[t:t + 1]
                for s in range(t + 1):
                    a = jnp.sum(q[t] * k[s] * jnp.exp(G[t] - G[s]), axis=-1, keepdims=True)
                    ot = ot + a * iv[s]
                o_ref[rows[t]:rows[t] + 1, cs] = ot
            st_new = jnp.exp(G[T - 1]) * st + _dot_tn(i_sc[...].astype(BF16), kd_sc[...].astype(BF16))
            so_ref[d, h] = st_new.T


def hgrn_sample(y, logf, kf, state, dbb=2):
    rb = dbb * DEC_SEQ
    r0 = P_ROWS // rb
    kern = functools.partial(_hgrn_sample_kernel, dbb=dbb)
    row = lambda c: (lambda i: (r0 + i, c))
    return pl.pallas_call(
        kern,
        grid=(DEC_BATCH // dbb,),
        in_specs=[
            pl.BlockSpec((rb, 512), row(C_HGQ // 512)),
            pl.BlockSpec((rb, 512), row(0)),
            pl.BlockSpec((rb, 512), row(0)),
            pl.BlockSpec((rb, 512), row(C_HGI // 512)),
            pl.BlockSpec((dbb, HG_HEADS, HG_DIM, HG_DIM), lambda i: (i, 0, 0, 0)),
        ],
        out_specs=[
            pl.BlockSpec((rb, 512), lambda i: (i, 0)),
            pl.BlockSpec((dbb, HG_HEADS, HG_DIM, HG_DIM), lambda i: (i, 0, 0, 0)),
        ],
        out_shape=[
            jax.ShapeDtypeStruct((S_ROWS, 512), F32),
            jax.ShapeDtypeStruct((DEC_BATCH, HG_HEADS, HG_DIM, HG_DIM), F32),
        ],
        scratch_shapes=[pltpu.VMEM((8, HG_DIM), F32), pltpu.VMEM((8, HG_DIM), F32), pltpu.VMEM((8, HG_DIM), F32)],
        compiler_params=_cp(("parallel",)),
        name="hgrn_sample",
    )(y, logf, kf, y, state)


def _headmm_kernel(a_ref, w_ref, o_ref):
    o_ref[...] = _dot(a_ref[...], w_ref[0]).astype(o_ref.dtype)


def q_latent(q_full, w_ukT):
    return pl.pallas_call(
        _headmm_kernel,
        grid=(MLA_HEADS,),
        in_specs=[pl.BlockSpec((S_ROWS, 128), lambda h: (P_ROWS // S_ROWS, 2 * h)),
                  pl.BlockSpec((1, 128, 256), lambda h: (h, 0, 0))],
        out_specs=pl.BlockSpec((S_ROWS, 256), lambda h: (0, h)),
        out_shape=jax.ShapeDtypeStruct((S_ROWS, MLA_HEADS * 256), BF16),
        compiler_params=_cp(("parallel",)),
        name="q_latent",
    )(q_full, w_ukT)


def o_latent_up(o_lat, w_uv3):
    return pl.pallas_call(
        _headmm_kernel,
        grid=(MLA_HEADS,),
        in_specs=[pl.BlockSpec((S_ROWS, 256), lambda h: (0, h)),
                  pl.BlockSpec((1, 256, 128), lambda h: (h, 0, 0))],
        out_specs=pl.BlockSpec((S_ROWS, 128), lambda h: (0, h)),
        out_shape=jax.ShapeDtypeStruct((S_ROWS, MLA_HEADS * MLA_V), BF16),
        compiler_params=_cp(("parallel",)),
        name="o_latent_up",
    )(o_lat, w_uv3)


def _mla_sample_kernel(pt_ref, ql_ref, qf_ref, cn_ref, rn_ref, *rest, pps):
    c_refs = rest[:pps]
    r_refs = rest[pps:2 * pps]
    o_ref = rest[2 * pps]
    m_sc, l_sc, acc_sc = rest[2 * pps + 1:]
    g = pl.program_id(1)
    ql = ql_ref[0]
    qr = qf_ref[0][:, 128:128 + MLA_ROPE]

    @pl.when(g == 0)
    def _():
        qlf = ql.astype(F32)
        qrf = qr.astype(F32)
        t_row = lax.broadcasted_iota(jnp.int32, (DEC_SEQ * MLA_HEADS, 1), 0) // MLA_HEADS
        cs, ss = [], []
        for s in range(DEC_SEQ):
            c = cn_ref[0, s:s + 1, :]
            r = rn_ref[0, s:s + 1, 0:MLA_ROPE]
            sc = (jnp.sum(qlf * c, axis=-1, keepdims=True) + jnp.sum(qrf * r, axis=-1, keepdims=True)) * MLA_SCALE
            ss.append(jnp.where(t_row >= s, sc, -jnp.inf))
            cs.append(c)
        m = ss[0]
        for s in range(1, DEC_SEQ):
            m = jnp.maximum(m, ss[s])
        l = jnp.zeros_like(m)
        acc = jnp.zeros(acc_sc.shape, F32)
        for s in range(DEC_SEQ):
            p = jnp.exp(ss[s] - m)
            l = l + p
            acc = acc + p * cs[s]
        m_sc[...] = m
        l_sc[...] = l
        acc_sc[...] = acc

    ss = []
    for p in range(pps):
        c = c_refs[p][0, 0].astype(BF16)
        r = r_refs[p][0, 0].astype(BF16)
        ss.append((_dot_nt(ql, c) + _dot_nt(qr, r)) * MLA_SCALE)
    m_prev = m_sc[...]
    m_new = m_prev
    for s in ss:
        m_new = jnp.maximum(m_new, jnp.max(s, axis=-1, keepdims=True))
    alpha = jnp.exp(m_prev - m_new)
    l = alpha * l_sc[...]
    acc = alpha * acc_sc[...]
    for p in range(pps):
        pr = jnp.exp(ss[p] - m_new)
        l = l + jnp.sum(pr, axis=-1, keepdims=True)
        acc = acc + _dot(pr.astype(BF16), c_refs[p][0, 0].astype(BF16))
    m_sc[...] = m_new
    l_sc[...] = l
    acc_sc[...] = acc

    @pl.when(g == pl.num_programs(1) - 1)
    def _():
        o_ref[0] = (acc / l).astype(o_ref.dtype)


def mla_sample(pt_flat, q_lat3, q_full3, c_new, r_new, cache_ckv, cache_kr, layer, pps=16):
    ng = N_PAGES // pps
    rows = DEC_SEQ * MLA_HEADS
    kern = functools.partial(_mla_sample_kernel, pps=pps)

    def page_map(p):
        return lambda b, g, pt: (layer, pt[b * N_PAGES + g * pps + p], 0, 0)

    per_b = lambda b, g, pt: (b, 0, 0)
    in_specs = [
        pl.BlockSpec((1, rows, 256), per_b),
        pl.BlockSpec((1, rows, 256), per_b),
        pl.BlockSpec((1, DEC_SEQ, 256), per_b),
        pl.BlockSpec((1, DEC_SEQ, 128), per_b),
    ]
    in_specs += [pl.BlockSpec((1, 1, PAGE, MLA_KV_LORA), page_map(p)) for p in range(pps)]
    in_specs += [pl.BlockSpec((1, 1, PAGE, MLA_ROPE), page_map(p)) for p in range(pps)]
    return pl.pallas_call(
        kern,
        grid_spec=pltpu.PrefetchScalarGridSpec(
            num_scalar_prefetch=1,
            grid=(DEC_BATCH, ng),
            in_specs=in_specs,
            out_specs=pl.BlockSpec((1, rows, 256), per_b),
            scratch_shapes=[pltpu.VMEM((rows, 1), F32), pltpu.VMEM((rows, 1), F32), pltpu.VMEM((rows, 256), F32)],
        ),
        out_shape=jax.ShapeDtypeStruct((DEC_BATCH, rows, 256), BF16),
        compiler_params=_cp(("parallel", "arbitrary")),
        name="mla_sample",
    )(pt_flat, q_lat3, q_full3, c_new, r_new, *([cache_ckv] * pps), *([cache_kr] * pps))


def _sb_sample_kernel(pt_ref, q_ref, kn_ref, vn_ref, *rest, pps):
    k_refs = rest[:pps]
    v_refs = rest[pps:2 * pps]
    o_ref = rest[2 * pps]
    carry_sc, acc_sc = rest[2 * pps + 1:]
    g = pl.program_id(1)
    rows = DEC_SEQ * SB_HEADS
    qf = q_ref[0]
    qb = qf.astype(BF16)

    @pl.when(g == 0)
    def _():
        t_row = lax.broadcasted_iota(jnp.int32, (rows, 1), 0) // SB_HEADS
        ls, neg, valid = [], [], []
        for s in range(DEC_SEQ):
            z = jnp.sum(qf * kn_ref[0, s:s + 1, :], axis=-1, keepdims=True) * SB_SCALE
            t = _softplus_neg_abs(z)
            ok = t_row > s
            valid.append(ok)
            ls.append(jnp.minimum(z, 0.0) - t)
            neg.append(jnp.where(ok, jnp.minimum(-z, 0.0) - t, 0.0))
        acc = jnp.zeros(acc_sc.shape, F32)
        after = jnp.zeros((rows, 1), F32)
        for s in range(DEC_SEQ - 1, -1, -1):
            a = jnp.where(valid[s], jnp.exp(ls[s] + after), 0.0)
            acc = acc + a * vn_ref[0, s:s + 1, :]
            after = after + neg[s]
        carry_sc[...] = after
        acc_sc[...] = acc

    ur = lax.broadcasted_iota(jnp.int32, (PAGE, PAGE), 0)
    uc = lax.broadcasted_iota(jnp.int32, (PAGE, PAGE), 1)
    later = (ur > uc).astype(BF16)
    carry = carry_sc[...]
    acc = acc_sc[...]
    for p in range(pps):
        k = k_refs[p][0, 0].astype(BF16)
        v = v_refs[p][0, 0].astype(BF16)
        z = _dot_nt(qb, k) * SB_SCALE
        t = _softplus_neg_abs(z)
        ls = jnp.minimum(z, 0.0) - t
        neg = jnp.minimum(-z, 0.0) - t
        hi, lo = _split2(neg)
        after = _dot(hi, later) + _dot(lo, later) + carry
        a = jnp.exp(ls + after)
        acc = acc + _dot(a.astype(BF16), v)
        carry = carry + jnp.sum(neg, axis=-1, keepdims=True)
    carry_sc[...] = carry
    acc_sc[...] = acc

    @pl.when(g == pl.num_programs(1) - 1)
    def _():
        o_ref[0] = acc.astype(o_ref.dtype)


def sb_sample(pt_flat, q3, k_new, v_new, cache_k, cache_v, layer, pps=16):
    ng = N_PAGES // pps
    rows = DEC_SEQ * SB_HEADS
    kern = functools.partial(_sb_sample_kernel, pps=pps)

    def page_map(p):
        return lambda b, g, pt: (layer, pt[b * N_PAGES + (N_PAGES - 1) - (g * pps + p)], 0, 0)

    per_b = lambda b, g, pt: (b, 0, 0)
    in_specs = [
        pl.BlockSpec((1, rows, SB_DIM), per_b),
        pl.BlockSpec((1, DEC_SEQ, SB_DIM), per_b),
        pl.BlockSpec((1, DEC_SEQ, SB_DIM), per_b),
    ]
    in_specs += [pl.BlockSpec((1, 1, PAGE, SB_DIM), page_map(p)) for p in range(pps)]
    in_specs += [pl.BlockSpec((1, 1, PAGE, SB_DIM), page_map(p)) for p in range(pps)]
    return pl.pallas_call(
        kern,
        grid_spec=pltpu.PrefetchScalarGridSpec(
            num_scalar_prefetch=1,
            grid=(DEC_BATCH, ng),
            in_specs=in_specs,
            out_specs=pl.BlockSpec((1, rows, SB_DIM), per_b),
            scratch_shapes=[pltpu.VMEM((rows, 1), F32), pltpu.VMEM((rows, SB_DIM), F32)],
        ),
        out_shape=jax.ShapeDtypeStruct((DEC_BATCH, rows, SB_DIM), BF16),
        compiler_params=_cp(("parallel", "arbitrary")),
        name="sb_sample",
    )(pt_flat, q3, k_new, v_new, *([cache_k] * pps), *([cache_v] * pps))


def _merge_kernel(mla_ref, sb_ref, hgo_ref, hgg_ref, gm_ref, gs_ref, gh_ref, hn_ref, lm_ref, ls_ref, lh_ref,
                  o_ref, hg_sc):
    @pl.when(pl.program_id(1) == 0)
    def _():
        gate = hgg_ref[...]
        act = gate * jax.nn.sigmoid(gate)
        for h in range(HG_HEADS):
            cs = slice(h * HG_DIM, (h + 1) * HG_DIM)
            hg_sc[:, cs] = (_rms(hgo_ref[:, cs], hn_ref[...]) * act[:, cs]).astype(BF16)

    m = (jax.nn.sigmoid(gm_ref[...]) * _dot(mla_ref[...], lm_ref[...])
         + jax.nn.sigmoid(gs_ref[...]) * _dot(sb_ref[...], ls_ref[...])
         + jax.nn.sigmoid(gh_ref[...]) * _dot(hg_sc[...], lh_ref[...]))
    o_ref[...] = m.astype(o_ref.dtype)


def merge(mla_o, sb_o, hg_o, y, hg_norm, lift_mla, lift_sb, lift_hg, bm=BM, bn=512):
    row = lambda c: (lambda i, j: (i, c))
    gate = lambda c0: (lambda i, j: (i, c0 // bn + j))
    wcol = lambda i, j: (0, j)
    return pl.pallas_call(
        _merge_kernel,
        grid=(M_ROWS // bm, D_MODEL // bn),
        in_specs=[
            pl.BlockSpec((bm, 1024), row(0)),
            pl.BlockSpec((bm, 512), row(0)),
            pl.BlockSpec((bm, 512), row(0)),
            pl.BlockSpec((bm, 512), row(C_HGG // 512)),
            pl.BlockSpec((bm, bn), gate(C_GMLA)),
            pl.BlockSpec((bm, bn), gate(C_GSB)),
            pl.BlockSpec((bm, bn), gate(C_GHG)),
            pl.BlockSpec((1, HG_DIM), lambda i, j: (0, 0)),
            pl.BlockSpec((1024, bn), wcol),
            pl.BlockSpec((512, bn), wcol),
            pl.BlockSpec((512, bn), wcol),
        ],
        out_specs=pl.BlockSpec((bm, bn), lambda i, j: (i, j)),
        out_shape=jax.ShapeDtypeStruct((M_ROWS, D_MODEL), BF16),
        scratch_shapes=[pltpu.VMEM((bm, 512), BF16)],
        compiler_params=_cp(("parallel", "arbitrary")),
        name="merge",
    )(mla_o, sb_o, hg_o, y, y, y, y, hg_norm.reshape(1, -1), lift_mla, lift_sb, lift_hg)


def _ffn_gu_kernel(h_ref, wg_ref, wu_ref, o_ref):
    h = h_ref[...]
    g = _dot(h, wg_ref[...])
    u = _dot(h, wu_ref[...])
    o_ref[...] = (g * jax.nn.sigmoid(g) * u).astype(o_ref.dtype)


def ffn_gate_up(h, wg, wu, bm=BM, bn=512):
    return pl.pallas_call(
        _ffn_gu_kernel,
        grid=(M_ROWS // bm, D_FF // bn),
        in_specs=[pl.BlockSpec((bm, D_MODEL), lambda i, j: (i, 0)),
                  pl.BlockSpec((D_MODEL, bn), lambda i, j: (0, j)),
                  pl.BlockSpec((D_MODEL, bn), lambda i, j: (0, j))],
        out_specs=pl.BlockSpec((bm, bn), lambda i, j: (i, j)),
        out_shape=jax.ShapeDtypeStruct((M_ROWS, D_FF), BF16),
        compiler_params=_cp(("parallel", "arbitrary")),
        name="ffn_gate_up",
    )(h, wg, wu)


def _rope_tables():
    half = MLA_ROPE // 2
    inv = ROPE_THETA ** (-jnp.arange(half, dtype=F32) / half)
    pos = jnp.concatenate([
        jnp.tile(jnp.arange(SEQ, dtype=jnp.int32), N_BATCH),
        N_PAGES * PAGE + jnp.tile(jnp.arange(DEC_SEQ, dtype=jnp.int32), DEC_BATCH),
    ]).astype(F32)
    ang = pos[:, None] * inv[None, :]
    pad = jnp.zeros((M_ROWS, 128 - MLA_ROPE), F32)
    cos = jnp.concatenate([jnp.cos(ang), jnp.cos(ang), pad], axis=-1)
    sin = jnp.concatenate([jnp.sin(ang), jnp.sin(ang), pad], axis=-1)
    return cos, sin


def _permute_w_in(w):
    pad = jnp.zeros((w.shape[0], N_IN - 9792), w.dtype)
    return jnp.concatenate([w[:, 0:512], w[:, 832:1344], w[:, 512:768], w[:, 1344:9792], w[:, 768:832], pad],
                           axis=-1).astype(BF16)


def kernel(x_prompt, x_sample, cache_mla_ckv, cache_mla_krope, cache_sb_k, cache_sb_v, state_hgrn, page_table, norm_attn, w_in, mla_q_norm, mla_w_uq, mla_kv_norm, mla_w_uk, mla_w_uv, hg_lb_logits, hg_norm, w_lift_mla, w_lift_sb, w_lift_hg, w_out, norm_ffn, ffn_w_gate, ffn_w_up, ffn_w_down, norm_final):
    cos, sin = _rope_tables()
    lb_all = lower_bounds(hg_lb_logits.astype(F32))
    pt_flat = page_table.reshape(-1).astype(jnp.int32)
    x = jnp.concatenate([x_prompt.reshape(P_ROWS, D_MODEL), x_sample.reshape(S_ROWS, D_MODEL)], axis=0)

    outs = {k: [] for k in ("p_ckv", "p_kr", "p_sk", "p_sv", "p_hs", "s_ckv", "s_kr", "s_sk", "s_sv", "s_hs")}
    for l in range(DEPTH):
        w_in_p = _permute_w_in(w_in[l])
        w_uq = jnp.pad(mla_w_uq[l].reshape(MLA_Q_LORA, MLA_HEADS, MLA_NOPE + MLA_ROPE),
                       ((0, 0), (0, 0), (0, MLA_QK - MLA_NOPE - MLA_ROPE))).reshape(MLA_Q_LORA, -1).astype(BF16)
        w_uk2 = mla_w_uk[l].reshape(MLA_KV_LORA, -1).astype(BF16)
        w_uv2 = mla_w_uv[l].reshape(MLA_KV_LORA, -1).astype(BF16)
        w_ukT = jnp.transpose(mla_w_uk[l], (1, 2, 0)).astype(BF16)
        w_uv3 = jnp.transpose(mla_w_uv[l], (1, 0, 2)).astype(BF16)

        h = rmsnorm(x, norm_attn[l], BF16)
        y = matmul(h, w_in_p, 768, F32, name="in_proj")
        q_full, ckv, kr, logf, kf = prep(y, cos, sin, mla_q_norm[l], w_uq, mla_kv_norm[l], lb_all[l])

        k_full, v = kv_up(ckv, kr, w_uk2, w_uv2)
        mla_p = mla_prompt(q_full, k_full, v)
        sb_p = sb_prompt(y)
        hg_p, hs_p = hgrn_prompt(y, logf, kf)

        q_lat = q_latent(q_full, w_ukT)
        o_lat = mla_sample(
            pt_flat,
            q_lat.reshape(DEC_BATCH, DEC_SEQ * MLA_HEADS, 256),
            q_full[P_ROWS:].reshape(DEC_BATCH, DEC_SEQ * MLA_HEADS, MLA_QK),
            ckv[P_ROWS:].reshape(DEC_BATCH, DEC_SEQ, MLA_KV_LORA),
            kr[P_ROWS:].reshape(DEC_BATCH, DEC_SEQ, 128),
            cache_mla_ckv, cache_mla_krope, l)
        mla_s = o_latent_up(o_lat.reshape(S_ROWS, MLA_HEADS * 256), w_uv3)
        sb_s = sb_sample(
            pt_flat,
            y[P_ROWS:, C_SBQ:C_SBQ + 512].reshape(DEC_BATCH, DEC_SEQ * SB_HEADS, SB_DIM),
            y[P_ROWS:, C_SBK:C_SBK + 128].reshape(DEC_BATCH, DEC_SEQ, SB_DIM),
            y[P_ROWS:, C_SBV:C_SBV + 128].reshape(DEC_BATCH, DEC_SEQ, SB_DIM),
            cache_sb_k, cache_sb_v, l)
        hg_s, hs_s = hgrn_sample(y, logf, kf, state_hgrn[l])

        mla_o = jnp.concatenate([mla_p, mla_s], axis=0)
        sb_o = jnp.concatenate([sb_p, sb_s.reshape(S_ROWS, SB_HEADS * SB_DIM)], axis=0)
        hg_o = jnp.concatenate([hg_p, hg_s], axis=0)
        m = merge(mla_o, sb_o, hg_o, y, hg_norm[l], w_lift_mla[l].astype(BF16), w_lift_sb[l].astype(BF16),
                  w_lift_hg[l].astype(BF16))
        x = matmul(m, w_out[l].astype(BF16), 512, F32, res=x, name="out_proj")
        h2 = rmsnorm(x, norm_ffn[l], BF16)
        act = ffn_gate_up(h2, ffn_w_gate[l].astype(BF16), ffn_w_up[l].astype(BF16))
        x = matmul(act, ffn_w_down[l].astype(BF16), 512, F32, res=x, vmem=VMEM_LIMIT_BIG, name="ffn_down")

        outs["p_ckv"].append(ckv[:P_ROWS].reshape(N_BATCH, SEQ, MLA_KV_LORA))
        outs["p_kr"].append(kr[:P_ROWS, :MLA_ROPE].reshape(N_BATCH, SEQ, MLA_ROPE))
        outs["p_sk"].append(y[:P_ROWS, C_SBK:C_SBK + 128].reshape(N_BATCH, SEQ, SB_DIM))
        outs["p_sv"].append(y[:P_ROWS, C_SBV:C_SBV + 128].reshape(N_BATCH, SEQ, SB_DIM))
        outs["p_hs"].append(hs_p.reshape(N_BATCH, HG_HEADS, HG_DIM, HG_DIM))
        outs["s_ckv"].append(ckv[P_ROWS:].reshape(DEC_BATCH, DEC_SEQ, MLA_KV_LORA))
        outs["s_kr"].append(kr[P_ROWS:, :MLA_ROPE].reshape(DEC_BATCH, DEC_SEQ, MLA_ROPE))
        outs["s_sk"].append(y[P_ROWS:, C_SBK:C_SBK + 128].reshape(DEC_BATCH, DEC_SEQ, SB_DIM))
        outs["s_sv"].append(y[P_ROWS:, C_SBV:C_SBV + 128].reshape(DEC_BATCH, DEC_SEQ, SB_DIM))
        outs["s_hs"].append(hs_s)

    yf = rmsnorm(x, norm_final, F32)
    y_prompt = yf[:P_ROWS].reshape(N_BATCH, SEQ, D_MODEL)
    y_sample = yf[P_ROWS:].reshape(DEC_BATCH, DEC_SEQ, D_MODEL)
    st = lambda k: jnp.stack(outs[k])
    return (y_prompt, y_sample, st("p_ckv"), st("p_kr"), st("p_sk"), st("p_sv"), st("p_hs"),
            st("s_ckv"), st("s_kr"), st("s_sk"), st("s_sv"), st("s_hs"))
```

```python
import functools

import jax
import jax.numpy as jnp
from jax import lax
from jax.experimental import pallas as pl
from jax.experimental.pallas import tpu as pltpu

F32 = jnp.float32
BF16 = jnp.bfloat16

D_MODEL = 2048
DEPTH = 4
N_BATCH = 2
SEQ = 4096
DEC_BATCH = 128
DEC_SEQ = 4
PAGE = 128
N_PAGES = 64
P_ROWS = N_BATCH * SEQ
S_ROWS = DEC_BATCH * DEC_SEQ
M_ROWS = P_ROWS + S_ROWS

MLA_HEADS = 8
MLA_NOPE = 128
MLA_ROPE = 64
MLA_V = 128
MLA_Q_LORA = 512
MLA_KV_LORA = 256
MLA_QK = 256
ROPE_THETA = 10000.0
SB_HEADS = 4
SB_DIM = 128
HG_HEADS = 4
HG_DIM = 128
HG_CHUNK = 64
D_FF = 5632
RMS_EPS = 1e-6

C_CQ, C_SBQ, C_CKV, C_SBK, C_SBV = 0, 512, 1024, 1280, 1408
C_HGQ, C_HGF, C_HGI, C_HGG = 1536, 2048, 2560, 3072
C_GMLA, C_GSB, C_GHG, C_KPE = 3584, 5632, 7680, 9728
N_IN = 9984

BM = 1088
VMEM_LIMIT = 48 * 1024 * 1024
VMEM_LIMIT_BIG = 56 * 1024 * 1024

SB_GROUP = 4
MLA_GROUP = 16
SB_DONE = -104.0

MLA_SCALE = (MLA_NOPE + MLA_ROPE) ** -0.5
SB_SCALE = SB_DIM ** -0.5
HG_SCALE = HG_DIM ** -0.5


def _cp(sem, vmem=VMEM_LIMIT):
    return pltpu.CompilerParams(dimension_semantics=sem, vmem_limit_bytes=vmem)


def _dot(a, b):
    return jnp.dot(a, b, preferred_element_type=F32)


def _dot_nt(a, b):
    return lax.dot_general(a, b, (((1,), (1,)), ((), ())), preferred_element_type=F32)


def _dot_tn(a, b):
    return lax.dot_general(a, b, (((0,), (0,)), ((), ())), preferred_element_type=F32)


def _tile_lanes(x, n):
    return jnp.tile(x, (1, n))


def _rms(x, g):
    return x * lax.rsqrt(jnp.mean(x * x, axis=-1, keepdims=True) + RMS_EPS) * g


def _softplus_neg_abs(z):
    return jnp.log1p(jnp.exp(-jnp.abs(z)))


def _split2(x):
    hi = x.astype(BF16)
    lo = (x - hi.astype(F32)).astype(BF16)
    return hi, lo


def _split3(x):
    hi = x.astype(BF16)
    r = x - hi.astype(F32)
    mid = r.astype(BF16)
    lo = (r - mid.astype(F32)).astype(BF16)
    return hi, mid, lo


def _norm_kernel(x_ref, g_ref, o_ref):
    o_ref[...] = _rms(x_ref[...], g_ref[...]).astype(o_ref.dtype)


def rmsnorm(x, g, out_dtype, bm=544):
    m, d = x.shape
    return pl.pallas_call(
        _norm_kernel,
        grid=(m // bm,),
        in_specs=[pl.BlockSpec((bm, d), lambda i: (i, 0)), pl.BlockSpec((1, d), lambda i: (0, 0))],
        out_specs=pl.BlockSpec((bm, d), lambda i: (i, 0)),
        out_shape=jax.ShapeDtypeStruct((m, d), out_dtype),
        compiler_params=_cp(("parallel",)),
        name="rmsnorm",
    )(x, g.reshape(1, d))


def _mm_kernel(a_ref, w_ref, o_ref):
    o_ref[...] = _dot(a_ref[...], w_ref[...]).astype(o_ref.dtype)


def _mm_res_kernel(a_ref, w_ref, r_ref, o_ref):
    o_ref[...] = r_ref[...] + _dot(a_ref[...], w_ref[...])


def matmul(a, w, bn, out_dtype, res=None, bm=BM, vmem=VMEM_LIMIT, name="matmul"):
    m, k = a.shape
    n = w.shape[1]
    in_specs = [pl.BlockSpec((bm, k), lambda i, j: (i, 0)), pl.BlockSpec((k, bn), lambda i, j: (0, j))]
    args = [a, w]
    body = _mm_kernel
    if res is not None:
        in_specs.append(pl.BlockSpec((bm, bn), lambda i, j: (i, j)))
        args.append(res)
        body = _mm_res_kernel
    return pl.pallas_call(
        body,
        grid=(m // bm, n // bn),
        in_specs=in_specs,
        out_specs=pl.BlockSpec((bm, bn), lambda i, j: (i, j)),
        out_shape=jax.ShapeDtypeStruct((m, n), out_dtype),
        compiler_params=_cp(("parallel", "arbitrary"), vmem),
        name=name,
    )(*args)


def _lb_kernel(x_ref, o_ref):
    x = x_ref[...]
    e = jnp.exp(x - jnp.max(x, axis=0, keepdims=True))
    p = e / jnp.sum(e, axis=0, keepdims=True)
    cum0 = p[0:1]
    cum = cum0
    o_ref[0:1, :] = cum - cum0
    for l in range(1, DEPTH):
        cum = cum + p[l:l + 1]
        o_ref[l:l + 1, :] = cum - cum0


def lower_bounds(logits):
    return pl.pallas_call(
        _lb_kernel, out_shape=jax.ShapeDtypeStruct(logits.shape, F32), name="hg_lower_bounds")(logits)


def _rope(x, cos, sin, first):
    rot = jnp.where(first, -pltpu.roll(x, 96, 1), pltpu.roll(x, 32, 1))
    return x * cos + rot * sin


def _prep_kernel(cq_ref, ckv_ref, kpe_ref, hf_ref, cos_ref, sin_ref, qn_ref, wuq_ref, kvn_ref, lb_ref,
                 q_ref, ckvo_ref, kro_ref, logf_ref, kf_ref):
    cos = cos_ref[...]
    sin = sin_ref[...]
    first = lax.broadcasted_iota(jnp.int32, cos.shape, 1) < (MLA_ROPE // 2)
    cqn = _rms(cq_ref[...], qn_ref[...]).astype(BF16)
    q = _dot(cqn, wuq_ref[...])
    for h in range(MLA_HEADS):
        a = h * MLA_QK
        q_ref[:, a:a + 128] = q[:, a:a + 128].astype(BF16)
        q_ref[:, a + 128:a + 256] = _rope(q[:, a + 128:a + 256], cos, sin, first).astype(BF16)
    ckvo_ref[...] = _rms(ckv_ref[...], kvn_ref[...])
    kro_ref[...] = _rope(kpe_ref[...], cos, sin, first)
    z = hf_ref[...]
    lb = lb_ref[...]
    t = _softplus_neg_abs(z)
    a_ = jnp.log(lb)
    b_ = jnp.log1p(-lb) + (jnp.minimum(z, 0.0) - t)
    logf_ref[...] = jnp.maximum(a_, b_) + jnp.log1p(jnp.exp(-jnp.abs(a_ - b_)))
    kf_ref[...] = (1.0 - lb) * jnp.exp(jnp.minimum(-z, 0.0) - t)


def prep(y, cos, sin, q_norm, w_uq, kv_norm, lb, bm=512):
    m = y.shape[0]
    row = lambda c: (lambda i: (i, c))
    const = lambda i: (0, 0)
    return pl.pallas_call(
        _prep_kernel,
        grid=(m // bm,),
        in_specs=[
            pl.BlockSpec((bm, 512), row(C_CQ // 512)),
            pl.BlockSpec((bm, 256), row(C_CKV // 256)),
            pl.BlockSpec((bm, 128), row(C_KPE // 128)),
            pl.BlockSpec((bm, 512), row(C_HGF // 512)),
            pl.BlockSpec((bm, 128), row(0)),
            pl.BlockSpec((bm, 128), row(0)),
            pl.BlockSpec((1, 512), const),
            pl.BlockSpec((MLA_Q_LORA, MLA_HEADS * MLA_QK), const),
            pl.BlockSpec((1, 256), const),
            pl.BlockSpec((1, 512), const),
        ],
        out_specs=[
            pl.BlockSpec((bm, MLA_HEADS * MLA_QK), row(0)),
            pl.BlockSpec((bm, 256), row(0)),
            pl.BlockSpec((bm, 128), row(0)),
            pl.BlockSpec((bm, 512), row(0)),
            pl.BlockSpec((bm, 512), row(0)),
        ],
        out_shape=[
            jax.ShapeDtypeStruct((m, MLA_HEADS * MLA_QK), BF16),
            jax.ShapeDtypeStruct((m, 256), F32),
            jax.ShapeDtypeStruct((m, 128), F32),
            jax.ShapeDtypeStruct((m, 512), F32),
            jax.ShapeDtypeStruct((m, 512), F32),
        ],
        compiler_params=_cp(("parallel",)),
        name="prep",
    )(y, y, y, y, cos, sin, q_norm.reshape(1, -1), w_uq, kv_norm.reshape(1, -1), lb.reshape(1, -1))


def _kvup_kernel(ckv_ref, kr_ref, wuk_ref, wuv_ref, k_ref, v_ref):
    c = ckv_ref[...].astype(BF16)
    kn = _dot(c, wuk_ref[...])
    v_ref[...] = _dot(c, wuv_ref[...]).astype(BF16)
    kr = kr_ref[...].astype(BF16)
    for h in range(MLA_HEADS):
        a = h * MLA_QK
        k_ref[:, a:a + 128] = kn[:, h * 128:(h + 1) * 128].astype(BF16)
        k_ref[:, a + 128:a + 256] = kr


def kv_up(ckv, kr, w_uk, w_uv, bm=512):
    const = lambda i: (0, 0)
    row = lambda i: (i, 0)
    return pl.pallas_call(
        _kvup_kernel,
        grid=(P_ROWS // bm,),
        in_specs=[pl.BlockSpec((bm, 256), row), pl.BlockSpec((bm, 128), row),
                  pl.BlockSpec((256, 1024), const), pl.BlockSpec((256, 1024), const)],
        out_specs=[pl.BlockSpec((bm, MLA_HEADS * MLA_QK), row), pl.BlockSpec((bm, 1024), row)],
        out_shape=[jax.ShapeDtypeStruct((P_ROWS, MLA_HEADS * MLA_QK), BF16),
                   jax.ShapeDtypeStruct((P_ROWS, 1024), BF16)],
        compiler_params=_cp(("parallel",)),
        name="kv_up",
    )(ckv, kr, w_uk, w_uv)


def _mla_flash_kernel(ti_ref, tj_ref, q_ref, k_ref, v_ref, o_ref, m_sc, l_sc, acc_sc, *, tq, tk):
    step_id = pl.program_id(1)
    i = ti_ref[step_id]
    j = tj_ref[step_id]
    j_last = (i * tq + tq - 1) // tk

    @pl.when(j == 0)
    def _():
        m_sc[...] = jnp.full(m_sc.shape, -jnp.inf, F32)
        l_sc[...] = jnp.zeros(l_sc.shape, F32)
        acc_sc[...] = jnp.zeros(acc_sc.shape, F32)

    def step(masked):
        if masked:
            row = i * tq + lax.broadcasted_iota(jnp.int32, (tq, tk), 0)
            col = j * tk + lax.broadcasted_iota(jnp.int32, (tq, tk), 1)
            keep = col <= row
        for h in range(MLA_HEADS):
            q = q_ref[:, h * MLA_QK:(h + 1) * MLA_QK]
            k = k_ref[:, h * MLA_QK:(h + 1) * MLA_QK]
            v = v_ref[:, h * MLA_V:(h + 1) * MLA_V]
            s = _dot_nt(q, k) * MLA_SCALE
            if masked:
                s = jnp.where(keep, s, -jnp.inf)
            m_prev = m_sc[h]
            m_new = jnp.maximum(m_prev, jnp.max(s, axis=-1, keepdims=True))
            alpha = jnp.exp(m_prev - m_new)
            p = jnp.exp(s - _tile_lanes(m_new, tk // 128))
            l_sc[h] = alpha * l_sc[h] + jnp.sum(p, axis=-1, keepdims=True)
            acc_sc[h] = alpha * acc_sc[h] + _dot(p.astype(BF16), v)
            m_sc[h] = m_new

    @pl.when(j < j_last)
    def _():
        step(False)

    @pl.when(j == j_last)
    def _():
        step(True)
        for h in range(MLA_HEADS):
            o_ref[:, h * MLA_V:(h + 1) * MLA_V] = (acc_sc[h] / l_sc[h]).astype(o_ref.dtype)


def mla_prompt(q_full, k_full, v, tq=256, tk=512):
    nq = SEQ // tq
    nk = SEQ // tk
    pairs = [(i, j) for i in range(nq) for j in range((i * tq + tq - 1) // tk + 1)]
    ti = jnp.asarray([p[0] for p in pairs], jnp.int32)
    tj = jnp.asarray([p[1] for p in pairs], jnp.int32)
    kern = functools.partial(_mla_flash_kernel, tq=tq, tk=tk)
    return pl.pallas_call(
        kern,
        grid_spec=pltpu.PrefetchScalarGridSpec(
            num_scalar_prefetch=2,
            grid=(N_BATCH, len(pairs)),
            in_specs=[
                pl.BlockSpec((tq, MLA_HEADS * MLA_QK), lambda b, s, ti, tj: (b * nq + ti[s], 0)),
                pl.BlockSpec((tk, MLA_HEADS * MLA_QK), lambda b, s, ti, tj: (b * nk + tj[s], 0)),
                pl.BlockSpec((tk, MLA_HEADS * MLA_V), lambda b, s, ti, tj: (b * nk + tj[s], 0)),
            ],
            out_specs=pl.BlockSpec((tq, MLA_HEADS * MLA_V), lambda b, s, ti, tj: (b * nq + ti[s], 0)),
            scratch_shapes=[pltpu.VMEM((MLA_HEADS, tq, 128), F32), pltpu.VMEM((MLA_HEADS, tq, 128), F32),
                            pltpu.VMEM((MLA_HEADS, tq, MLA_V), F32)],
        ),
        out_shape=jax.ShapeDtypeStruct((P_ROWS, MLA_HEADS * MLA_V), BF16),
        compiler_params=_cp(("parallel", "arbitrary")),
        name="mla_prompt",
    )(ti, tj, q_full, k_full, v)


def _sb_prompt_kernel(q_ref, k_ref, v_ref, o_ref, carry_sc, acc_sc, *, tq, tk):
    i = pl.program_id(1)
    j = pl.program_id(2)
    ratio = tq // tk
    last = ratio * i + ratio - 1
    jj = last - j

    @pl.when(j == 0)
    def _():
        carry_sc[...] = jnp.zeros(carry_sc.shape, F32)
        acc_sc[...] = jnp.zeros(acc_sc.shape, F32)

    def step(masked):
        k = k_ref[...].astype(BF16)
        v = v_ref[...].astype(BF16)
        ur = lax.broadcasted_iota(jnp.int32, (tk, tk), 0)
        uc = lax.broadcasted_iota(jnp.int32, (tk, tk), 1)
        later = (ur > uc).astype(BF16)
        if masked:
            row = i * tq + lax.broadcasted_iota(jnp.int32, (tq, tk), 0)
            col = jj * tk + lax.broadcasted_iota(jnp.int32, (tq, tk), 1)
            valid = col < row
        for h in range(SB_HEADS):
            q = q_ref[:, h * SB_DIM:(h + 1) * SB_DIM].astype(BF16)
            z = _dot_nt(q, k) * SB_SCALE
            t = _softplus_neg_abs(z)
            ls = jnp.minimum(z, 0.0) - t
            neg = jnp.minimum(-z, 0.0) - t
            if masked:
                neg = jnp.where(valid, neg, 0.0)
            hi, lo = _split2(neg)
            after = _dot(hi, later) + _dot(lo, later) + carry_sc[h]
            a = jnp.exp(ls + after)
            if masked:
                a = jnp.where(valid, a, 0.0)
            acc_sc[h] = acc_sc[h] + _dot(a.astype(BF16), v)
            carry_sc[h] = carry_sc[h] + jnp.sum(neg, axis=-1, keepdims=True)

    @pl.when(j < ratio)
    def _():
        step(True)

    @pl.when(jnp.logical_and(j <= last, j >= ratio))
    def _():
        @pl.when(jnp.max(carry_sc[...]) >= SB_DONE)
        def _():
            step(False)

    @pl.when(j == last)
    def _():
        for h in range(SB_HEADS):
            o_ref[:, h * SB_DIM:(h + 1) * SB_DIM] = acc_sc[h].astype(o_ref.dtype)


def sb_prompt(y, tq=512, tk=256):
    nq = SEQ // tq
    nk = SEQ // tk
    ratio = tq // tk
    kern = functools.partial(_sb_prompt_kernel, tq=tq, tk=tk)

    def kv_map(c):
        def f(b, i, j):
            jj = jnp.maximum(ratio * i + ratio - 1 - j, 0)
            return (b * nk + jj, c)
        return f

    return pl.pallas_call(
        kern,
        grid=(N_BATCH, nq, nk),
        in_specs=[
            pl.BlockSpec((tq, SB_HEADS * SB_DIM), lambda b, i, j: (b * nq + i, C_SBQ // 512)),
            pl.BlockSpec((tk, SB_DIM), kv_map(C_SBK // 128)),
            pl.BlockSpec((tk, SB_DIM), kv_map(C_SBV // 128)),
        ],
        out_specs=pl.BlockSpec((tq, SB_HEADS * SB_DIM), lambda b, i, j: (b * nq + i, 0)),
        out_shape=jax.ShapeDtypeStruct((P_ROWS, SB_HEADS * SB_DIM), BF16),
        scratch_shapes=[pltpu.VMEM((SB_HEADS, tq, 1), F32), pltpu.VMEM((SB_HEADS, tq, SB_DIM), F32)],
        compiler_params=_cp(("parallel", "parallel", "arbitrary")),
        name="sb_prompt",
    )(y, y, y)


def _hgrn_prompt_kernel(q_ref, g_ref, kf_ref, i_ref, o_ref, so_ref, st_sc, dg_sc, *, nchunk):
    tstep = pl.program_id(1)
    C = HG_CHUNK
    SUB = 16

    @pl.when(tstep == 0)
    def _():
        st_sc[...] = jnp.zeros(st_sc.shape, F32)

    lr = lax.broadcasted_iota(jnp.int32, (C, C), 0)
    lc = lax.broadcasted_iota(jnp.int32, (C, C), 1)
    incl = (lc <= lr).astype(BF16)
    sr = lax.broadcasted_iota(jnp.int32, (SUB, HG_DIM), 0)

    def chunk_head(base, h):
        r = pl.ds(base, C)
        cs = slice(h * HG_DIM, (h + 1) * HG_DIM)
        g = g_ref[r, cs]
        qf = q_ref[r, cs] * HG_SCALE
        kf = kf_ref[r, cs]
        ii = i_ref[r, cs]
        iib = ii.astype(BF16)
        hi, mid, lo = _split3(g)
        G = _dot(incl, hi) + _dot(incl, mid) + _dot(incl, lo)
        st = st_sc[h]
        o = _dot_nt((qf * jnp.exp(G)).astype(BF16), st.astype(BF16))

        def offdiag(t0, t1, s0, s1):
            gm = G[s1 - 1:s1, :]
            qt = (qf[t0:t1] * jnp.exp(G[t0:t1] - gm)).astype(BF16)
            kt = (kf[s0:s1] * jnp.exp(gm - G[s0:s1])).astype(BF16)
            a = _dot_nt(qt, kt)
            return _dot(a.astype(BF16), iib[s0:s1])

        off_a = offdiag(32, 64, 0, 32)
        off_b0 = offdiag(16, 32, 0, 16)
        off_b1 = offdiag(48, 64, 32, 48)

        for blk in range(C // SUB):
            b0 = blk * SUB
            gb = G[b0:b0 + SUB]
            kb = kf[b0:b0 + SUB]
            ib = ii[b0:b0 + SUB]
            for t in range(SUB):
                keep = sr <= t
                dec = jnp.exp(jnp.where(keep, G[b0 + t:b0 + t + 1] - gb, 0.0))
                w = jnp.where(keep, qf[b0 + t:b0 + t + 1] * kb * dec, 0.0)
                a_col = jnp.sum(w, axis=-1, keepdims=True)
                dg_sc[h, b0 + t:b0 + t + 1, :] = jnp.sum(a_col * ib, axis=0, keepdims=True)

        dg = dg_sc[h]
        o_ref[pl.ds(base, 16), cs] = o[0:16] + dg[0:16]
        o_ref[pl.ds(base + 16, 16), cs] = o[16:32] + dg[16:32] + off_b0
        o_ref[pl.ds(base + 32, 16), cs] = o[32:48] + dg[32:48] + off_a[0:16]
        o_ref[pl.ds(base + 48, 16), cs] = o[48:64] + dg[48:64] + off_a[16:32] + off_b1

        g_last = G[C - 1:C, :]
        kd = (kf * jnp.exp(g_last - G)).astype(BF16)
        st_sc[h] = jnp.exp(g_last) * st + _dot_tn(iib, kd)

    def chunk(c, carry):
        base = pl.multiple_of(c * C, C)
        for h in range(HG_HEADS):
            chunk_head(base, h)
        return carry

    lax.fori_loop(0, nchunk, chunk, 0)

    @pl.when(tstep == pl.num_programs(1) - 1)
    def _():
        for h in range(HG_HEADS):
            so_ref[h] = st_sc[h].T


def hgrn_prompt(y, logf, kf, tb=512):
    nt = SEQ // tb
    width = HG_HEADS * HG_DIM
    kern = functools.partial(_hgrn_prompt_kernel, nchunk=tb // HG_CHUNK)
    col = lambda c0: (lambda b, t: (b * nt + t, c0))
    return pl.pallas_call(
        kern,
        grid=(N_BATCH, nt),
        in_specs=[
            pl.BlockSpec((tb, width), col(C_HGQ // width)),
            pl.BlockSpec((tb, width), col(0)),
            pl.BlockSpec((tb, width), col(0)),
            pl.BlockSpec((tb, width), col(C_HGI // width)),
        ],
        out_specs=[
            pl.BlockSpec((tb, width), col(0)),
            pl.BlockSpec((HG_HEADS, HG_DIM, HG_DIM), lambda b, t: (b, 0, 0)),
        ],
        out_shape=[
            jax.ShapeDtypeStruct((P_ROWS, width), F32),
            jax.ShapeDtypeStruct((N_BATCH * HG_HEADS, HG_DIM, HG_DIM), F32),
        ],
        scratch_shapes=[pltpu.VMEM((HG_HEADS, HG_DIM, HG_DIM), F32), pltpu.VMEM((HG_HEADS, HG_CHUNK, HG_DIM), F32)],
        compiler_params=_cp(("parallel", "arbitrary")),
        name="hgrn_prompt",
    )(y, logf, kf, y)


def _hgrn_sample_kernel(q_ref, g_ref, kf_ref, i_ref, s_ref, o_ref, so_ref, lhs_sc, kd_sc, i_sc, *, dbb):
    T = DEC_SEQ
    for d in range(dbb):
        for h in range(HG_HEADS):
            cs = slice(h * HG_DIM, (h + 1) * HG_DIM)
            rows = [d * T + t for t in range(T)]
            q = [q_ref[r:r + 1, cs] * HG_SCALE for r in rows]
            g = [g_ref[r:r + 1, cs] for r in rows]
            k = [kf_ref[r:r + 1, cs] for r in rows]
            iv = [i_ref[r:r + 1, cs] for r in rows]
            G = [g[0]]
            for t in range(1, T):
                G.append(G[t - 1] + g[t])
            st = s_ref[d, h].T
            lhs_sc[...] = jnp.zeros(lhs_sc.shape, F32)
            kd_sc[...] = jnp.zeros(kd_sc.shape, F32)
            i_sc[...] = jnp.zeros(i_sc.shape, F32)
            for t in range(T):
                lhs_sc[t:t + 1, :] = q[t] * jnp.exp(G[t])
                kd_sc[t:t + 1, :] = k[t] * jnp.exp(G[T - 1] - G[t])
                i_sc[t:t + 1, :] = iv[t]
            o = _dot_nt(lhs_sc[...].astype(BF16), st.astype(BF16))
            for t in range(T):
                ot = o[t:t + 1]
                for s in range(t + 1):
                    a = jnp.sum(q[t] * k[s] * jnp.exp(G[t] - G[s]), axis=-1, keepdims=True)
                    ot = ot + a * iv[s]
                o_ref[rows[t]:rows[t] + 1, cs] = ot
            st_new = jnp.exp(G[T - 1]) * st + _dot_tn(i_sc[...].astype(BF16), kd_sc[...].astype(BF16))
            so_ref[d, h] = st_new.T


def hgrn_sample(y, logf, kf, state, dbb=2):
    rb = dbb * DEC_SEQ
    r0 = P_ROWS // rb
    kern = functools.partial(_hgrn_sample_kernel, dbb=dbb)
    row = lambda c: (lambda i: (r0 + i, c))
    return pl.pallas_call(
        kern,
        grid=(DEC_BATCH // dbb,),
        in_specs=[
            pl.BlockSpec((rb, 512), row(C_HGQ // 512)),
            pl.BlockSpec((rb, 512), row(0)),
            pl.BlockSpec((rb, 512), row(0)),
            pl.BlockSpec((rb, 512), row(C_HGI // 512)),
            pl.BlockSpec((dbb, HG_HEADS, HG_DIM, HG_DIM), lambda i: (i, 0, 0, 0)),
        ],
        out_specs=[
            pl.BlockSpec((rb, 512), lambda i: (i, 0)),
            pl.BlockSpec((dbb, HG_HEADS, HG_DIM, HG_DIM), lambda i: (i, 0, 0, 0)),
        ],
        out_shape=[
            jax.ShapeDtypeStruct((S_ROWS, 512), F32),
            jax.ShapeDtypeStruct((DEC_BATCH, HG_HEADS, HG_DIM, HG_DIM), F32),
        ],
        scratch_shapes=[pltpu.VMEM((8, HG_DIM), F32), pltpu.VMEM((8, HG_DIM), F32), pltpu.VMEM((8, HG_DIM), F32)],
        compiler_params=_cp(("parallel",)),
        name="hgrn_sample",
    )(y, logf, kf, y, state)


def _headmm_kernel(a_ref, w_ref, o_ref):
    o_ref[...] = _dot(a_ref[...], w_ref[0]).astype(o_ref.dtype)


def q_latent(q_full, w_ukT):
    return pl.pallas_call(
        _headmm_kernel,
        grid=(MLA_HEADS,),
        in_specs=[pl.BlockSpec((S_ROWS, 128), lambda h: (P_ROWS // S_ROWS, 2 * h)),
                  pl.BlockSpec((1, 128, 256), lambda h: (h, 0, 0))],
        out_specs=pl.BlockSpec((S_ROWS, 256), lambda h: (0, h)),
        out_shape=jax.ShapeDtypeStruct((S_ROWS, MLA_HEADS * 256), BF16),
        compiler_params=_cp(("parallel",)),
        name="q_latent",
    )(q_full, w_ukT)


def o_latent_up(o_lat, w_uv3):
    return pl.pallas_call(
        _headmm_kernel,
        grid=(MLA_HEADS,),
        in_specs=[pl.BlockSpec((S_ROWS, 256), lambda h: (0, h)),
                  pl.BlockSpec((1, 256, 128), lambda h: (h, 0, 0))],
        out_specs=pl.BlockSpec((S_ROWS, 128), lambda h: (0, h)),
        out_shape=jax.ShapeDtypeStruct((S_ROWS, MLA_HEADS * MLA_V), BF16),
        compiler_params=_cp(("parallel",)),
        name="o_latent_up",
    )(o_lat, w_uv3)


def _mla_sample_kernel(pt_ref, ql_ref, qf_ref, cn_ref, rn_ref, cc_hbm, cr_hbm, o_ref,
                       cbuf, rbuf, csem, rsem, m_sc, l_sc, acc_sc, *, layer):
    b = pl.program_id(0)
    g = pl.program_id(1)
    ng = pl.num_programs(1)
    n = b * ng + g
    slot = lax.rem(n, 2)

    def group_copies(bb, gg, sl):
        cps = []
        for p in range(MLA_GROUP):
            page = pt_ref[bb * N_PAGES + gg * MLA_GROUP + p]
            cps.append(pltpu.make_async_copy(
                cc_hbm.at[layer, page], cbuf.at[sl, pl.ds(p * PAGE, PAGE)], csem.at[sl]))
            cps.append(pltpu.make_async_copy(cr_hbm.at[layer, page], rbuf.at[sl, p], rsem.at[sl]))
        return cps

    @pl.when(n == 0)
    def _():
        for cp in group_copies(0, 0, 0):
            cp.start()

    @pl.when(n + 1 < pl.num_programs(0) * ng)
    def _():
        for cp in group_copies((n + 1) // ng, lax.rem(n + 1, ng), 1 - slot):
            cp.start()

    ql = ql_ref[0]
    qr = qf_ref[0][:, 128:128 + MLA_ROPE]

    @pl.when(g == 0)
    def _():
        qlf = ql.astype(F32)
        qrf = qr.astype(F32)
        t_row = lax.broadcasted_iota(jnp.int32, (DEC_SEQ * MLA_HEADS, 1), 0) // MLA_HEADS
        cs, ss = [], []
        for s in range(DEC_SEQ):
            c = cn_ref[0, s:s + 1, :]
            r = rn_ref[0, s:s + 1, 0:MLA_ROPE]
            sc = (jnp.sum(qlf * c, axis=-1, keepdims=True) + jnp.sum(qrf * r, axis=-1, keepdims=True)) * MLA_SCALE
            ss.append(jnp.where(t_row >= s, sc, -jnp.inf))
            cs.append(c)
        m = ss[0]
        for s in range(1, DEC_SEQ):
            m = jnp.maximum(m, ss[s])
        l = jnp.zeros_like(m)
        acc = jnp.zeros(acc_sc.shape, F32)
        for s in range(DEC_SEQ):
            p = jnp.exp(ss[s] - m)
            l = l + p
            acc = acc + p * cs[s]
        m_sc[...] = jnp.broadcast_to(m, m_sc.shape)
        l_sc[...] = jnp.broadcast_to(l, l_sc.shape)
        acc_sc[...] = acc

    for cp in group_copies(b, g, slot):
        cp.wait()

    keys = MLA_GROUP * PAGE
    cb = cbuf[slot].astype(BF16)
    s_rope = jnp.concatenate(
        [_dot(qr, rbuf[slot, p].astype(BF16)) for p in range(MLA_GROUP)], axis=1)
    s = (_dot_nt(ql, cb) + s_rope) * MLA_SCALE
    m_prev = m_sc[...]
    m_new = jnp.maximum(m_prev, jnp.max(s, axis=-1, keepdims=True))
    alpha = jnp.exp(m_prev - m_new)
    pr = jnp.exp(s - _tile_lanes(m_new, keys // 128))
    l_new = alpha * l_sc[...] + jnp.sum(pr, axis=-1, keepdims=True)
    acc_new = _tile_lanes(alpha, 2) * acc_sc[...] + _dot(pr.astype(BF16), cb)
    m_sc[...] = m_new
    l_sc[...] = l_new
    acc_sc[...] = acc_new

    @pl.when(g == ng - 1)
    def _():
        o_ref[0] = (acc_new / _tile_lanes(l_new, 2)).astype(o_ref.dtype)


def mla_sample(pt_flat, q_lat3, q_full3, c_new, r_new, cache_ckv, cache_krT, layer):
    ng = N_PAGES // MLA_GROUP
    rows = DEC_SEQ * MLA_HEADS
    kern = functools.partial(_mla_sample_kernel, layer=layer)
    per_b = lambda b, g, pt: (b, 0, 0)
    return pl.pallas_call(
        kern,
        grid_spec=pltpu.PrefetchScalarGridSpec(
            num_scalar_prefetch=1,
            grid=(DEC_BATCH, ng),
            in_specs=[
                pl.BlockSpec((1, rows, 256), per_b),
                pl.BlockSpec((1, rows, 256), per_b),
                pl.BlockSpec((1, DEC_SEQ, 256), per_b),
                pl.BlockSpec((1, DEC_SEQ, 128), per_b),
                pl.BlockSpec(memory_space=pl.ANY),
                pl.BlockSpec(memory_space=pl.ANY),
            ],
            out_specs=pl.BlockSpec((1, rows, 256), per_b),
            scratch_shapes=[
                pltpu.VMEM((2, MLA_GROUP * PAGE, MLA_KV_LORA), F32),
                pltpu.VMEM((2, MLA_GROUP, MLA_ROPE, PAGE), F32),
                pltpu.SemaphoreType.DMA((2,)),
                pltpu.SemaphoreType.DMA((2,)),
                pltpu.VMEM((rows, 128), F32), pltpu.VMEM((rows, 128), F32), pltpu.VMEM((rows, 256), F32),
            ],
        ),
        out_shape=jax.ShapeDtypeStruct((DEC_BATCH, rows, 256), BF16),
        compiler_params=_cp(("arbitrary", "arbitrary")),
        name="mla_sample",
    )(pt_flat, q_lat3, q_full3, c_new, r_new, cache_ckv, cache_krT)


def _sb_sample_kernel(pt_ref, q_ref, kn_ref, vn_ref, later_ref, ck_hbm, cv_hbm, o_ref,
                      kbuf, vbuf, ksem, vsem, *, layer):
    b = pl.program_id(0)
    slot = lax.rem(b, 2)
    rows = DEC_SEQ * SB_HEADS
    n_groups = N_PAGES // SB_GROUP

    def group_copies(bb, gg, sl):
        cps = []
        for p in range(SB_GROUP):
            page = pt_ref[bb * N_PAGES + (N_PAGES - 1) - (gg * SB_GROUP + p)]
            dst = pl.ds((SB_GROUP - 1 - p) * PAGE, PAGE)
            cps.append(pltpu.make_async_copy(ck_hbm.at[layer, page], kbuf.at[sl, dst], ksem.at[sl]))
            cps.append(pltpu.make_async_copy(cv_hbm.at[layer, page], vbuf.at[sl, dst], vsem.at[sl]))
        return cps

    @pl.when(b == 0)
    def _():
        for cp in group_copies(0, 0, 0):
            cp.start()

    @pl.when(b + 1 < pl.num_programs(0))
    def _():
        for cp in group_copies(b + 1, 0, 1 - slot):
            cp.start()

    qf = q_ref[0]
    qb = qf.astype(BF16)

    t_row = lax.broadcasted_iota(jnp.int32, (rows, 1), 0) // SB_HEADS
    ls, neg, valid = [], [], []
    for s in range(DEC_SEQ):
        z = jnp.sum(qf * kn_ref[0, s:s + 1, :], axis=-1, keepdims=True) * SB_SCALE
        t = _softplus_neg_abs(z)
        ok = t_row > s
        valid.append(ok)
        ls.append(jnp.minimum(z, 0.0) - t)
        neg.append(jnp.where(ok, jnp.minimum(-z, 0.0) - t, 0.0))
    acc = jnp.zeros((rows, SB_DIM), F32)
    carry = jnp.zeros((rows, 1), F32)
    for s in range(DEC_SEQ - 1, -1, -1):
        a = jnp.where(valid[s], jnp.exp(ls[s] + carry), 0.0)
        acc = acc + a * vn_ref[0, s:s + 1, :]
        carry = carry + neg[s]

    def process(sl, carry, acc):
        k = kbuf[sl].astype(BF16)
        v = vbuf[sl].astype(BF16)
        z = _dot_nt(qb, k) * SB_SCALE
        t = _softplus_neg_abs(z)
        ls = jnp.minimum(z, 0.0) - t
        neg = jnp.minimum(-z, 0.0) - t
        hi, lo = _split2(neg)
        later = later_ref[...]
        after = _dot(hi, later) + _dot(lo, later) + carry
        a = jnp.exp(ls + after)
        return carry + jnp.sum(neg, axis=-1, keepdims=True), acc + _dot(a.astype(BF16), v)

    for cp in group_copies(b, 0, slot):
        cp.wait()
    carry, acc = process(slot, carry, acc)

    def more(st):
        return jnp.logical_and(st[0] < n_groups, jnp.max(st[1]) >= SB_DONE)

    def fetch_and_process(st):
        gg, carry, acc = st
        cps = group_copies(b, gg, 2)
        for cp in cps:
            cp.start()
        for cp in cps:
            cp.wait()
        carry, acc = process(2, carry, acc)
        return gg + 1, carry, acc

    _, _, acc = lax.while_loop(more, fetch_and_process, (jnp.int32(1), carry, acc))
    o_ref[0] = acc.astype(o_ref.dtype)


def sb_sample(pt_flat, q3, k_new, v_new, cache_k, cache_v, layer):
    rows = DEC_SEQ * SB_HEADS
    keys = SB_GROUP * PAGE
    kern = functools.partial(_sb_sample_kernel, layer=layer)
    later = (jnp.arange(keys)[:, None] > jnp.arange(keys)[None, :]).astype(BF16)
    per_b = lambda b, pt: (b, 0, 0)
    return pl.pallas_call(
        kern,
        grid_spec=pltpu.PrefetchScalarGridSpec(
            num_scalar_prefetch=1,
            grid=(DEC_BATCH,),
            in_specs=[
                pl.BlockSpec((1, rows, SB_DIM), per_b),
                pl.BlockSpec((1, DEC_SEQ, SB_DIM), per_b),
                pl.BlockSpec((1, DEC_SEQ, SB_DIM), per_b),
                pl.BlockSpec((keys, keys), lambda b, pt: (0, 0)),
                pl.BlockSpec(memory_space=pl.ANY),
                pl.BlockSpec(memory_space=pl.ANY),
            ],
            out_specs=pl.BlockSpec((1, rows, SB_DIM), per_b),
            scratch_shapes=[
                pltpu.VMEM((3, keys, SB_DIM), F32),
                pltpu.VMEM((3, keys, SB_DIM), F32),
                pltpu.SemaphoreType.DMA((3,)),
                pltpu.SemaphoreType.DMA((3,)),
            ],
        ),
        out_shape=jax.ShapeDtypeStruct((DEC_BATCH, rows, SB_DIM), BF16),
        compiler_params=_cp(("arbitrary",)),
        name="sb_sample",
    )(pt_flat, q3, k_new, v_new, later, cache_k, cache_v)


def _merge_kernel(mla_p_ref, sb_p_ref, hgo_p_ref, mla_s_ref, sb_s_ref, hgo_s_ref, hgg_ref, gm_ref, gs_ref, gh_ref,
                  hn_ref, lm_ref, ls_ref, lh_ref, o_ref, mla_sc, sb_sc, hg_sc):
    is_sample = pl.program_id(0) == pl.num_programs(0) - 1
    first = pl.program_id(1) == 0

    def stage(mla_ref, sb_ref, hgo_ref):
        mla_sc[...] = mla_ref[...]
        sb_sc[...] = sb_ref[...]
        gate = hgg_ref[...]
        act = gate * jax.nn.sigmoid(gate)
        for h in range(HG_HEADS):
            cs = slice(h * HG_DIM, (h + 1) * HG_DIM)
            hg_sc[:, cs] = (_rms(hgo_ref[:, cs], hn_ref[...]) * act[:, cs]).astype(BF16)

    @pl.when(jnp.logical_and(first, jnp.logical_not(is_sample)))
    def _():
        stage(mla_p_ref, sb_p_ref, hgo_p_ref)

    @pl.when(jnp.logical_and(first, is_sample))
    def _():
        stage(mla_s_ref, sb_s_ref, hgo_s_ref)

    m = (jax.nn.sigmoid(gm_ref[...]) * _dot(mla_sc[...], lm_ref[...])
         + jax.nn.sigmoid(gs_ref[...]) * _dot(sb_sc[...], ls_ref[...])
         + jax.nn.sigmoid(gh_ref[...]) * _dot(hg_sc[...], lh_ref[...]))
    o_ref[...] = m.astype(o_ref.dtype)


def merge(mla_p, sb_p, hg_p, mla_s, sb_s, hg_s, y, hg_norm, lift_mla, lift_sb, lift_hg, bn=512):
    bm = S_ROWS
    n_prompt = P_ROWS // bm
    prow = lambda i, j: (jnp.minimum(i, n_prompt - 1), 0)
    srow = lambda i, j: (0, 0)
    row = lambda c: (lambda i, j: (i, c))
    gate = lambda c0: (lambda i, j: (i, c0 // bn + j))
    wcol = lambda i, j: (0, j)
    return pl.pallas_call(
        _merge_kernel,
        grid=(M_ROWS // bm, D_MODEL // bn),
        in_specs=[
            pl.BlockSpec((bm, 1024), prow),
            pl.BlockSpec((bm, 512), prow),
            pl.BlockSpec((bm, 512), prow),
            pl.BlockSpec((bm, 1024), srow),
            pl.BlockSpec((bm, 512), srow),
            pl.BlockSpec((bm, 512), srow),
            pl.BlockSpec((bm, 512), row(C_HGG // 512)),
            pl.BlockSpec((bm, bn), gate(C_GMLA)),
            pl.BlockSpec((bm, bn), gate(C_GSB)),
            pl.BlockSpec((bm, bn), gate(C_GHG)),
            pl.BlockSpec((1, HG_DIM), lambda i, j: (0, 0)),
            pl.BlockSpec((1024, bn), wcol),
            pl.BlockSpec((512, bn), wcol),
            pl.BlockSpec((512, bn), wcol),
        ],
        out_specs=pl.BlockSpec((bm, bn), lambda i, j: (i, j)),
        out_shape=jax.ShapeDtypeStruct((M_ROWS, D_MODEL), BF16),
        scratch_shapes=[pltpu.VMEM((bm, 1024), BF16), pltpu.VMEM((bm, 512), BF16), pltpu.VMEM((bm, 512), BF16)],
        compiler_params=_cp(("parallel", "arbitrary")),
        name="merge",
    )(mla_p, sb_p, hg_p, mla_s, sb_s, hg_s, y, y, y, y, hg_norm.reshape(1, -1), lift_mla, lift_sb, lift_hg)


def _ffn_gu_kernel(h_ref, wg_ref, wu_ref, o_ref):
    h = h_ref[...]
    g = _dot(h, wg_ref[...])
    u = _dot(h, wu_ref[...])
    o_ref[...] = (g * jax.nn.sigmoid(g) * u).astype(o_ref.dtype)


def ffn_gate_up(h, wg, wu, bm=BM, bn=512):
    return pl.pallas_call(
        _ffn_gu_kernel,
        grid=(M_ROWS // bm, D_FF // bn),
        in_specs=[pl.BlockSpec((bm, D_MODEL), lambda i, j: (i, 0)),
                  pl.BlockSpec((D_MODEL, bn), lambda i, j: (0, j)),
                  pl.BlockSpec((D_MODEL, bn), lambda i, j: (0, j))],
        out_specs=pl.BlockSpec((bm, bn), lambda i, j: (i, j)),
        out_shape=jax.ShapeDtypeStruct((M_ROWS, D_FF), BF16),
        compiler_params=_cp(("parallel", "arbitrary")),
        name="ffn_gate_up",
    )(h, wg, wu)


def _rope_tables():
    half = MLA_ROPE // 2
    inv = ROPE_THETA ** (-jnp.arange(half, dtype=F32) / half)
    pos = jnp.concatenate([
        jnp.tile(jnp.arange(SEQ, dtype=jnp.int32), N_BATCH),
        N_PAGES * PAGE + jnp.tile(jnp.arange(DEC_SEQ, dtype=jnp.int32), DEC_BATCH),
    ]).astype(F32)
    ang = pos[:, None] * inv[None, :]
    pad = jnp.zeros((M_ROWS, 128 - MLA_ROPE), F32)
    cos = jnp.concatenate([jnp.cos(ang), jnp.cos(ang), pad], axis=-1)
    sin = jnp.concatenate([jnp.sin(ang), jnp.sin(ang), pad], axis=-1)
    return cos, sin


def _permute_w_in(w):
    pad = jnp.zeros((w.shape[0], N_IN - 9792), w.dtype)
    return jnp.concatenate([w[:, 0:512], w[:, 832:1344], w[:, 512:768], w[:, 1344:9792], w[:, 768:832], pad],
                           axis=-1).astype(BF16)


def kernel(x_prompt, x_sample, cache_mla_ckv, cache_mla_krope, cache_sb_k, cache_sb_v, state_hgrn, page_table, norm_attn, w_in, mla_q_norm, mla_w_uq, mla_kv_norm, mla_w_uk, mla_w_uv, hg_lb_logits, hg_norm, w_lift_mla, w_lift_sb, w_lift_hg, w_out, norm_ffn, ffn_w_gate, ffn_w_up, ffn_w_down, norm_final):
    cos, sin = _rope_tables()
    lb_all = lower_bounds(hg_lb_logits.astype(F32))
    pt_flat = page_table.reshape(-1).astype(jnp.int32)
    cache_krT = jnp.swapaxes(cache_mla_krope, 2, 3)
    x = jnp.concatenate([x_prompt.reshape(P_ROWS, D_MODEL), x_sample.reshape(S_ROWS, D_MODEL)], axis=0)

    outs = {k: [] for k in ("p_ckv", "p_kr", "p_sk", "p_sv", "p_hs", "s_ckv", "s_kr", "s_sk", "s_sv", "s_hs")}
    for l in range(DEPTH):
        w_in_p = _permute_w_in(w_in[l])
        w_uq = jnp.pad(mla_w_uq[l].reshape(MLA_Q_LORA, MLA_HEADS, MLA_NOPE + MLA_ROPE),
                       ((0, 0), (0, 0), (0, MLA_QK - MLA_NOPE - MLA_ROPE))).reshape(MLA_Q_LORA, -1).astype(BF16)
        w_uk2 = mla_w_uk[l].reshape(MLA_KV_LORA, -1).astype(BF16)
        w_uv2 = mla_w_uv[l].reshape(MLA_KV_LORA, -1).astype(BF16)
        w_ukT = jnp.transpose(mla_w_uk[l], (1, 2, 0)).astype(BF16)
        w_uv3 = jnp.transpose(mla_w_uv[l], (1, 0, 2)).astype(BF16)

        h = rmsnorm(x, norm_attn[l], BF16)
        y = matmul(h, w_in_p, 768, F32, name="in_proj")
        q_full, ckv, kr, logf, kf = prep(y, cos, sin, mla_q_norm[l], w_uq, mla_kv_norm[l], lb_all[l])

        k_full, v = kv_up(ckv, kr, w_uk2, w_uv2)
        mla_p = mla_prompt(q_full, k_full, v)
        sb_p = sb_prompt(y)
        hg_p, hs_p = hgrn_prompt(y, logf, kf)

        q_lat = q_latent(q_full, w_ukT)
        o_lat = mla_sample(
            pt_flat,
            q_lat.reshape(DEC_BATCH, DEC_SEQ * MLA_HEADS, 256),
            q_full[P_ROWS:].reshape(DEC_BATCH, DEC_SEQ * MLA_HEADS, MLA_QK),
            ckv[P_ROWS:].reshape(DEC_BATCH, DEC_SEQ, MLA_KV_LORA),
            kr[P_ROWS:].reshape(DEC_BATCH, DEC_SEQ, 128),
            cache_mla_ckv, cache_krT, l)
        mla_s = o_latent_up(o_lat.reshape(S_ROWS, MLA_HEADS * 256), w_uv3)
        sb_s = sb_sample(
            pt_flat,
            y[P_ROWS:, C_SBQ:C_SBQ + 512].reshape(DEC_BATCH, DEC_SEQ * SB_HEADS, SB_DIM),
            y[P_ROWS:, C_SBK:C_SBK + 128].reshape(DEC_BATCH, DEC_SEQ, SB_DIM),
            y[P_ROWS:, C_SBV:C_SBV + 128].reshape(DEC_BATCH, DEC_SEQ, SB_DIM),
            cache_sb_k, cache_sb_v, l)
        hg_s, hs_s = hgrn_sample(y, logf, kf, state_hgrn[l])

        m = merge(mla_p, sb_p, hg_p, mla_s, sb_s.reshape(S_ROWS, SB_HEADS * SB_DIM), hg_s, y, hg_norm[l], w_lift_mla[l].astype(BF16), w_lift_sb[l].astype(BF16),
                  w_lift_hg[l].astype(BF16))
        x = matmul(m, w_out[l].astype(BF16), 512, F32, res=x, name="out_proj")
        h2 = rmsnorm(x, norm_ffn[l], BF16)
        act = ffn_gate_up(h2, ffn_w_gate[l].astype(BF16), ffn_w_up[l].astype(BF16))
        x = matmul(act, ffn_w_down[l].astype(BF16), 512, F32, res=x, vmem=VMEM_LIMIT_BIG, name="ffn_down")

        outs["p_ckv"].append(ckv[:P_ROWS].reshape(N_BATCH, SEQ, MLA_KV_LORA))
        outs["p_kr"].append(kr[:P_ROWS, :MLA_ROPE].reshape(N_BATCH, SEQ, MLA_ROPE))
        outs["p_sk"].append(y[:P_ROWS, C_SBK:C_SBK + 128].reshape(N_BATCH, SEQ, SB_DIM))
        outs["p_sv"].append(y[:P_ROWS, C_SBV:C_SBV + 128].reshape(N_BATCH, SEQ, SB_DIM))
        outs["p_hs"].append(hs_p.reshape(N_BATCH, HG_HEADS, HG_DIM, HG_DIM))
        outs["s_ckv"].append(ckv[P_ROWS:].reshape(DEC_BATCH, DEC_SEQ, MLA_KV_LORA))
        outs["s_kr"].append(kr[P_ROWS:, :MLA_ROPE].reshape(DEC_BATCH, DEC_SEQ, MLA_ROPE))
        outs["s_sk"].append(y[P_ROWS:, C_SBK:C_SBK + 128].reshape(DEC_BATCH, DEC_SEQ, SB_DIM))
        outs["s_sv"].append(y[P_ROWS:, C_SBV:C_SBV + 128].reshape(DEC_BATCH, DEC_SEQ, SB_DIM))
        outs["s_hs"].append(hs_s)

    yf = rmsnorm(x, norm_final, F32)
    y_prompt = yf[:P_ROWS].reshape(N_BATCH, SEQ, D_MODEL)
    y_sample = yf[P_ROWS:].reshape(DEC_BATCH, DEC_SEQ, D_MODEL)
    st = lambda k: jnp.stack(outs[k])
    return (y_prompt, y_sample, st("p_ckv"), st("p_kr"), st("p_sk"), st("p_sv"), st("p_hs"),
            st("s_ckv"), st("s_kr"), st("s_sk"), st("s_sv"), st("s_hs"))
```

```python
import functools

import jax
import jax.numpy as jnp
from jax import lax
from jax.experimental import pallas as pl
from jax.experimental.pallas import tpu as pltpu

F32 = jnp.float32
BF16 = jnp.bfloat16

D_MODEL = 2048
DEPTH = 4
N_BATCH = 2
SEQ = 4096
DEC_BATCH = 128
DEC_SEQ = 4
PAGE = 128
N_PAGES = 64
P_ROWS = N_BATCH * SEQ
S_ROWS = DEC_BATCH * DEC_SEQ
M_ROWS = P_ROWS + S_ROWS

MLA_HEADS = 8
MLA_NOPE = 128
MLA_ROPE = 64
MLA_V = 128
MLA_Q_LORA = 512
MLA_KV_LORA = 256
MLA_QK = 256
ROPE_THETA = 10000.0
SB_HEADS = 4
SB_DIM = 128
HG_HEADS = 4
HG_DIM = 128
HG_CHUNK = 64
D_FF = 5632
RMS_EPS = 1e-6

C_CQ, C_SBQ, C_CKV, C_SBK, C_SBV = 0, 512, 1024, 1280, 1408
C_HGQ, C_HGF, C_HGI, C_HGG = 1536, 2048, 2560, 3072
C_GMLA, C_GSB, C_GHG, C_KPE = 3584, 5632, 7680, 9728
N_IN = 9984

BM = 1088
VMEM_LIMIT = 48 * 1024 * 1024
VMEM_LIMIT_BIG = 56 * 1024 * 1024

SB_GROUP = 4
MLA_GROUP = 32
MLA_SUB = 8
SB_DONE = -104.0

MLA_SCALE = (MLA_NOPE + MLA_ROPE) ** -0.5
MLA_SCALE_LOG2E = MLA_SCALE * 1.4426950408889634
SB_SCALE = SB_DIM ** -0.5
HG_SCALE = HG_DIM ** -0.5


def _cp(sem, vmem=VMEM_LIMIT):
    return pltpu.CompilerParams(dimension_semantics=sem, vmem_limit_bytes=vmem)


def _dot(a, b):
    return jnp.dot(a, b, preferred_element_type=F32)


def _dot_nt(a, b):
    return lax.dot_general(a, b, (((1,), (1,)), ((), ())), preferred_element_type=F32)


def _dot_tn(a, b):
    return lax.dot_general(a, b, (((0,), (0,)), ((), ())), preferred_element_type=F32)


def _tile_lanes(x, n):
    return jnp.tile(x, (1, n))


def _sigmoid(x):
    return 0.5 * jnp.tanh(0.5 * x) + 0.5


def _rms(x, g):
    return x * lax.rsqrt(jnp.mean(x * x, axis=-1, keepdims=True) + RMS_EPS) * g


def _softplus_neg_abs(z):
    return jnp.log1p(jnp.exp(-jnp.abs(z)))


def _split2(x):
    hi = x.astype(BF16)
    lo = (x - hi.astype(F32)).astype(BF16)
    return hi, lo


def _split3(x):
    hi = x.astype(BF16)
    r = x - hi.astype(F32)
    mid = r.astype(BF16)
    lo = (r - mid.astype(F32)).astype(BF16)
    return hi, mid, lo


def _norm_kernel(x_ref, g_ref, o_ref):
    o_ref[...] = _rms(x_ref[...], g_ref[...]).astype(o_ref.dtype)


def rmsnorm(x, g, out_dtype, bm=544):
    m, d = x.shape
    return pl.pallas_call(
        _norm_kernel,
        grid=(m // bm,),
        in_specs=[pl.BlockSpec((bm, d), lambda i: (i, 0)), pl.BlockSpec((1, d), lambda i: (0, 0))],
        out_specs=pl.BlockSpec((bm, d), lambda i: (i, 0)),
        out_shape=jax.ShapeDtypeStruct((m, d), out_dtype),
        compiler_params=_cp(("parallel",)),
        name="rmsnorm",
    )(x, g.reshape(1, d))


def _mm_kernel(a_ref, w_ref, o_ref):
    o_ref[...] = _dot(a_ref[...], w_ref[...]).astype(o_ref.dtype)


def _mm_res_kernel(a_ref, w_ref, r_ref, o_ref):
    o_ref[...] = r_ref[...] + _dot(a_ref[...], w_ref[...])


def _norm_mm_kernel(x_ref, g_ref, w_ref, o_ref, h_sc):
    @pl.when(pl.program_id(1) == 0)
    def _():
        h_sc[...] = _rms(x_ref[...], g_ref[...]).astype(BF16)

    o_ref[...] = _dot(h_sc[...], w_ref[...]).astype(o_ref.dtype)


def norm_matmul(x, g, w, bn, out_dtype, bm=BM, name="norm_matmul"):
    m, k = x.shape
    n = w.shape[1]
    return pl.pallas_call(
        _norm_mm_kernel,
        grid=(m // bm, n // bn),
        in_specs=[pl.BlockSpec((bm, k), lambda i, j: (i, 0)),
                  pl.BlockSpec((1, k), lambda i, j: (0, 0)),
                  pl.BlockSpec((k, bn), lambda i, j: (0, j))],
        out_specs=pl.BlockSpec((bm, bn), lambda i, j: (i, j)),
        out_shape=jax.ShapeDtypeStruct((m, n), out_dtype),
        scratch_shapes=[pltpu.VMEM((bm, k), BF16)],
        compiler_params=_cp(("parallel", "arbitrary")),
        name=name,
    )(x, g.reshape(1, k), w)


def matmul(a, w, layer, bn, out_dtype, res=None, bm=BM, vmem=VMEM_LIMIT, name="matmul"):
    m, k = a.shape
    n = w.shape[2]
    in_specs = [pl.BlockSpec((bm, k), lambda i, j: (i, 0)), pl.BlockSpec((None, k, bn), lambda i, j: (layer, 0, j))]
    args = [a, w]
    body = _mm_kernel
    if res is not None:
        in_specs.append(pl.BlockSpec((bm, bn), lambda i, j: (i, j)))
        args.append(res)
        body = _mm_res_kernel
    return pl.pallas_call(
        body,
        grid=(m // bm, n // bn),
        in_specs=in_specs,
        out_specs=pl.BlockSpec((bm, bn), lambda i, j: (i, j)),
        out_shape=jax.ShapeDtypeStruct((m, n), out_dtype),
        compiler_params=_cp(("parallel", "arbitrary"), vmem),
        name=name,
    )(*args)


def _lb_kernel(x_ref, o_ref):
    x = x_ref[...]
    e = jnp.exp(x - jnp.max(x, axis=0, keepdims=True))
    p = e / jnp.sum(e, axis=0, keepdims=True)
    cum0 = p[0:1]
    cum = cum0
    o_ref[0:1, :] = cum - cum0
    for l in range(1, DEPTH):
        cum = cum + p[l:l + 1]
        o_ref[l:l + 1, :] = cum - cum0


def lower_bounds(logits):
    return pl.pallas_call(
        _lb_kernel, out_shape=jax.ShapeDtypeStruct(logits.shape, F32), name="hg_lower_bounds")(logits)


def _rope(x, cos, sin, first):
    rot = jnp.where(first, -pltpu.roll(x, 96, 1), pltpu.roll(x, 32, 1))
    return x * cos + rot * sin


def _prep_kernel(cq_ref, ckv_ref, kpe_ref, hf_ref, cos_ref, sin_ref, qn_ref, wuq_ref, kvn_ref, lb_ref,
                 q_ref, ckvo_ref, kro_ref, logf_ref, kf_ref):
    cos = cos_ref[...]
    sin = sin_ref[...]
    first = lax.broadcasted_iota(jnp.int32, cos.shape, 1) < (MLA_ROPE // 2)
    cqn = _rms(cq_ref[...], qn_ref[...]).astype(BF16)
    q = _dot(cqn, wuq_ref[...])
    for h in range(MLA_HEADS):
        a = h * MLA_QK
        q_ref[:, a:a + 128] = q[:, a:a + 128].astype(BF16)
        q_ref[:, a + 128:a + 256] = _rope(q[:, a + 128:a + 256], cos, sin, first).astype(BF16)
    ckvo_ref[...] = _rms(ckv_ref[...], kvn_ref[...])
    kro_ref[...] = _rope(kpe_ref[...], cos, sin, first)
    z = hf_ref[...]
    lb = lb_ref[...]
    t = _softplus_neg_abs(z)
    a_ = jnp.log(lb)
    b_ = jnp.log1p(-lb) + (jnp.minimum(z, 0.0) - t)
    logf_ref[...] = jnp.maximum(a_, b_) + jnp.log1p(jnp.exp(-jnp.abs(a_ - b_)))
    kf_ref[...] = (1.0 - lb) * jnp.exp(jnp.minimum(-z, 0.0) - t)


def prep(y, cos, sin, q_norm, w_uq, kv_norm, lb, bm=512):
    m = y.shape[0]
    row = lambda c: (lambda i: (i, c))
    const = lambda i: (0, 0)
    return pl.pallas_call(
        _prep_kernel,
        grid=(m // bm,),
        in_specs=[
            pl.BlockSpec((bm, 512), row(C_CQ // 512)),
            pl.BlockSpec((bm, 256), row(C_CKV // 256)),
            pl.BlockSpec((bm, 128), row(C_KPE // 128)),
            pl.BlockSpec((bm, 512), row(C_HGF // 512)),
            pl.BlockSpec((bm, 128), row(0)),
            pl.BlockSpec((bm, 128), row(0)),
            pl.BlockSpec((1, 512), const),
            pl.BlockSpec((MLA_Q_LORA, MLA_HEADS * MLA_QK), const),
            pl.BlockSpec((1, 256), const),
            pl.BlockSpec((1, 512), const),
        ],
        out_specs=[
            pl.BlockSpec((bm, MLA_HEADS * MLA_QK), row(0)),
            pl.BlockSpec((bm, 256), row(0)),
            pl.BlockSpec((bm, 128), row(0)),
            pl.BlockSpec((bm, 512), row(0)),
            pl.BlockSpec((bm, 512), row(0)),
        ],
        out_shape=[
            jax.ShapeDtypeStruct((m, MLA_HEADS * MLA_QK), BF16),
            jax.ShapeDtypeStruct((m, 256), F32),
            jax.ShapeDtypeStruct((m, 128), F32),
            jax.ShapeDtypeStruct((m, 512), F32),
            jax.ShapeDtypeStruct((m, 512), F32),
        ],
        compiler_params=_cp(("parallel",)),
        name="prep",
    )(y, y, y, y, cos, sin, q_norm.reshape(1, -1), w_uq, kv_norm.reshape(1, -1), lb.reshape(1, -1))


def _kvup_kernel(ckv_ref, kr_ref, wuk_ref, wuv_ref, k_ref, v_ref):
    c = ckv_ref[...].astype(BF16)
    kn = _dot(c, wuk_ref[...])
    v_ref[...] = _dot(c, wuv_ref[...]).astype(BF16)
    kr = kr_ref[...].astype(BF16)
    for h in range(MLA_HEADS):
        a = h * MLA_QK
        k_ref[:, a:a + 128] = kn[:, h * 128:(h + 1) * 128].astype(BF16)
        k_ref[:, a + 128:a + 256] = kr


def kv_up(ckv, kr, w_uk, w_uv, bm=512):
    const = lambda i: (0, 0)
    row = lambda i: (i, 0)
    return pl.pallas_call(
        _kvup_kernel,
        grid=(P_ROWS // bm,),
        in_specs=[pl.BlockSpec((bm, 256), row), pl.BlockSpec((bm, 128), row),
                  pl.BlockSpec((256, 1024), const), pl.BlockSpec((256, 1024), const)],
        out_specs=[pl.BlockSpec((bm, MLA_HEADS * MLA_QK), row), pl.BlockSpec((bm, 1024), row)],
        out_shape=[jax.ShapeDtypeStruct((P_ROWS, MLA_HEADS * MLA_QK), BF16),
                   jax.ShapeDtypeStruct((P_ROWS, 1024), BF16)],
        compiler_params=_cp(("parallel",)),
        name="kv_up",
    )(ckv, kr, w_uk, w_uv)


def _mla_flash_kernel(ti_ref, tj_ref, q_ref, k_ref, v_ref, o_ref, m_sc, l_sc, acc_sc, *, tq, tk):
    step_id = pl.program_id(1)
    i = ti_ref[step_id]
    j = tj_ref[step_id]
    j_last = (i * tq + tq - 1) // tk

    @pl.when(j == 0)
    def _():
        m_sc[...] = jnp.full(m_sc.shape, -jnp.inf, F32)
        l_sc[...] = jnp.zeros(l_sc.shape, F32)
        acc_sc[...] = jnp.zeros(acc_sc.shape, F32)

    def step(masked):
        if masked:
            row = i * tq + lax.broadcasted_iota(jnp.int32, (tq, tk), 0)
            col = j * tk + lax.broadcasted_iota(jnp.int32, (tq, tk), 1)
            keep = col <= row
        for h in range(MLA_HEADS):
            q = q_ref[:, h * MLA_QK:(h + 1) * MLA_QK]
            k = k_ref[:, h * MLA_QK:(h + 1) * MLA_QK]
            v = v_ref[:, h * MLA_V:(h + 1) * MLA_V]
            s = _dot_nt(q, k)
            if masked:
                s = jnp.where(keep, s, -jnp.inf)
            m_prev = m_sc[h]
            m_new = jnp.maximum(m_prev, jnp.max(s, axis=-1, keepdims=True))
            alpha = jnp.exp2((m_prev - m_new) * MLA_SCALE_LOG2E)
            p = jnp.exp2((s - _tile_lanes(m_new, tk // 128)) * MLA_SCALE_LOG2E)
            l_sc[h] = alpha * l_sc[h] + jnp.sum(p, axis=-1, keepdims=True)
            acc_sc[h] = alpha * acc_sc[h] + _dot(p.astype(BF16), v)
            m_sc[h] = m_new

    @pl.when(j < j_last)
    def _():
        step(False)

    @pl.when(j == j_last)
    def _():
        step(True)
        for h in range(MLA_HEADS):
            o_ref[:, h * MLA_V:(h + 1) * MLA_V] = (acc_sc[h] / l_sc[h]).astype(o_ref.dtype)


def mla_prompt(q_full, k_full, v, tq=256, tk=512):
    nq = SEQ // tq
    nk = SEQ // tk
    pairs = [(i, j) for i in range(nq) for j in range((i * tq + tq - 1) // tk + 1)]
    ti = jnp.asarray([p[0] for p in pairs], jnp.int32)
    tj = jnp.asarray([p[1] for p in pairs], jnp.int32)
    kern = functools.partial(_mla_flash_kernel, tq=tq, tk=tk)
    return pl.pallas_call(
        kern,
        grid_spec=pltpu.PrefetchScalarGridSpec(
            num_scalar_prefetch=2,
            grid=(N_BATCH, len(pairs)),
            in_specs=[
                pl.BlockSpec((tq, MLA_HEADS * MLA_QK), lambda b, s, ti, tj: (b * nq + ti[s], 0)),
                pl.BlockSpec((tk, MLA_HEADS * MLA_QK), lambda b, s, ti, tj: (b * nk + tj[s], 0)),
                pl.BlockSpec((tk, MLA_HEADS * MLA_V), lambda b, s, ti, tj: (b * nk + tj[s], 0)),
            ],
            out_specs=pl.BlockSpec((tq, MLA_HEADS * MLA_V), lambda b, s, ti, tj: (b * nq + ti[s], 0)),
            scratch_shapes=[pltpu.VMEM((MLA_HEADS, tq, 128), F32), pltpu.VMEM((MLA_HEADS, tq, 128), F32),
                            pltpu.VMEM((MLA_HEADS, tq, MLA_V), F32)],
        ),
        out_shape=jax.ShapeDtypeStruct((P_ROWS, MLA_HEADS * MLA_V), BF16),
        compiler_params=_cp(("parallel", "arbitrary")),
        name="mla_prompt",
    )(ti, tj, q_full, k_full, v)


def _sb_prompt_kernel(q_ref, k_ref, v_ref, o_ref, kb_sc, vb_sc, carry_sc, acc_sc, *, tq, tk):
    i = pl.program_id(1)
    ratio = tq // tk
    last = ratio * i + ratio - 1

    @pl.when(i == 0)
    def _():
        kb_sc[...] = k_ref[...].astype(BF16)
        vb_sc[...] = v_ref[...].astype(BF16)

    carry_sc[...] = jnp.zeros(carry_sc.shape, F32)
    acc_sc[...] = jnp.zeros(acc_sc.shape, F32)
    ur = lax.broadcasted_iota(jnp.int32, (tk, tk), 0)
    uc = lax.broadcasted_iota(jnp.int32, (tk, tk), 1)
    later = (ur > uc).astype(BF16)

    def step(jj, masked):
        rows = pl.ds(pl.multiple_of(jj * tk, tk), tk)
        k = kb_sc[rows, :]
        v = vb_sc[rows, :]
        if masked:
            row = i * tq + lax.broadcasted_iota(jnp.int32, (tq, tk), 0)
            col = jj * tk + lax.broadcasted_iota(jnp.int32, (tq, tk), 1)
            valid = col < row
        for h in range(SB_HEADS):
            q = q_ref[:, h * SB_DIM:(h + 1) * SB_DIM].astype(BF16)
            z = _dot_nt(q, k) * SB_SCALE
            t = _softplus_neg_abs(z)
            ls = jnp.minimum(z, 0.0) - t
            neg = jnp.minimum(-z, 0.0) - t
            if masked:
                neg = jnp.where(valid, neg, 0.0)
            hi, lo = _split2(neg)
            after = _dot(hi, later) + _dot(lo, later) + carry_sc[h]
            a = jnp.exp(ls + after)
            if masked:
                a = jnp.where(valid, a, 0.0)
            acc_sc[h] = acc_sc[h] + _dot(a.astype(BF16), v)
            carry_sc[h] = carry_sc[h] + jnp.sum(neg, axis=-1, keepdims=True)

    for m in range(ratio):
        step(last - m, True)

    def live():
        return (jnp.max(carry_sc[...]) >= SB_DONE).astype(jnp.int32)

    def more(st):
        return jnp.logical_and(st[0] >= 0, st[1] > 0)

    def older(st):
        step(st[0], False)
        return st[0] - 1, live()

    lax.while_loop(more, older, (ratio * i - 1, live()))

    for h in range(SB_HEADS):
        o_ref[:, h * SB_DIM:(h + 1) * SB_DIM] = acc_sc[h].astype(o_ref.dtype)


def sb_prompt(y, tq=512, tk=256):
    nq = SEQ // tq
    kern = functools.partial(_sb_prompt_kernel, tq=tq, tk=tk)
    return pl.pallas_call(
        kern,
        grid=(N_BATCH, nq),
        in_specs=[
            pl.BlockSpec((tq, SB_HEADS * SB_DIM), lambda b, i: (b * nq + i, C_SBQ // 512)),
            pl.BlockSpec((SEQ, SB_DIM), lambda b, i: (b, C_SBK // 128)),
            pl.BlockSpec((SEQ, SB_DIM), lambda b, i: (b, C_SBV // 128)),
        ],
        out_specs=pl.BlockSpec((tq, SB_HEADS * SB_DIM), lambda b, i: (b * nq + i, 0)),
        out_shape=jax.ShapeDtypeStruct((P_ROWS, SB_HEADS * SB_DIM), BF16),
        scratch_shapes=[pltpu.VMEM((SEQ, SB_DIM), BF16), pltpu.VMEM((SEQ, SB_DIM), BF16),
                        pltpu.VMEM((SB_HEADS, tq, 1), F32), pltpu.VMEM((SB_HEADS, tq, SB_DIM), F32)],
        compiler_params=_cp(("arbitrary", "arbitrary")),
        name="sb_prompt",
    )(y, y, y)


def _hgrn_prompt_kernel(q_ref, g_ref, kf_ref, i_ref, o_ref, so_ref, st_sc, dg_sc, *, nchunk):
    tstep = pl.program_id(1)
    C = HG_CHUNK
    SUB = 16

    @pl.when(tstep == 0)
    def _():
        st_sc[...] = jnp.zeros(st_sc.shape, F32)

    lr = lax.broadcasted_iota(jnp.int32, (C, C), 0)
    lc = lax.broadcasted_iota(jnp.int32, (C, C), 1)
    incl = (lc <= lr).astype(BF16)
    sr = lax.broadcasted_iota(jnp.int32, (SUB, HG_DIM), 0)

    def chunk_head(base, h):
        r = pl.ds(base, C)
        cs = slice(h * HG_DIM, (h + 1) * HG_DIM)
        g = g_ref[r, cs]
        qf = q_ref[r, cs] * HG_SCALE
        kf = kf_ref[r, cs]
        ii = i_ref[r, cs]
        iib = ii.astype(BF16)
        hi, mid, lo = _split3(g)
        G = _dot(incl, hi) + _dot(incl, mid) + _dot(incl, lo)
        st = st_sc[h]
        o = _dot_nt((qf * jnp.exp(G)).astype(BF16), st.astype(BF16))

        def offdiag(t0, t1, s0, s1):
            gm = G[s1 - 1:s1, :]
            qt = (qf[t0:t1] * jnp.exp(G[t0:t1] - gm)).astype(BF16)
            kt = (kf[s0:s1] * jnp.exp(gm - G[s0:s1])).astype(BF16)
            a = _dot_nt(qt, kt)
            return _dot(a.astype(BF16), iib[s0:s1])

        off_a = offdiag(32, 64, 0, 32)
        off_b0 = offdiag(16, 32, 0, 16)
        off_b1 = offdiag(48, 64, 32, 48)

        for blk in range(C // SUB):
            b0 = blk * SUB
            gb = G[b0:b0 + SUB]
            kb = kf[b0:b0 + SUB]
            ib = ii[b0:b0 + SUB]
            for t in range(SUB):
                keep = sr <= t
                dec = jnp.exp(jnp.where(keep, G[b0 + t:b0 + t + 1] - gb, 0.0))
                w = jnp.where(keep, qf[b0 + t:b0 + t + 1] * kb * dec, 0.0)
                a_col = jnp.sum(w, axis=-1, keepdims=True)
                dg_sc[h, b0 + t:b0 + t + 1, :] = jnp.sum(a_col * ib, axis=0, keepdims=True)

        dg = dg_sc[h]
        o_ref[pl.ds(base, 16), cs] = o[0:16] + dg[0:16]
        o_ref[pl.ds(base + 16, 16), cs] = o[16:32] + dg[16:32] + off_b0
        o_ref[pl.ds(base + 32, 16), cs] = o[32:48] + dg[32:48] + off_a[0:16]
        o_ref[pl.ds(base + 48, 16), cs] = o[48:64] + dg[48:64] + off_a[16:32] + off_b1

        g_last = G[C - 1:C, :]
        kd = (kf * jnp.exp(g_last - G)).astype(BF16)
        st_sc[h] = jnp.exp(g_last) * st + _dot_tn(iib, kd)

    def chunk(c, carry):
        base = pl.multiple_of(c * C, C)
        for h in range(HG_HEADS):
            chunk_head(base, h)
        return carry

    lax.fori_loop(0, nchunk, chunk, 0, unroll=2)

    @pl.when(tstep == pl.num_programs(1) - 1)
    def _():
        for h in range(HG_HEADS):
            so_ref[h] = st_sc[h].T


def hgrn_prompt(y, logf, kf, tb=512):
    nt = SEQ // tb
    width = HG_HEADS * HG_DIM
    kern = functools.partial(_hgrn_prompt_kernel, nchunk=tb // HG_CHUNK)
    col = lambda c0: (lambda b, t: (b * nt + t, c0))
    return pl.pallas_call(
        kern,
        grid=(N_BATCH, nt),
        in_specs=[
            pl.BlockSpec((tb, width), col(C_HGQ // width)),
            pl.BlockSpec((tb, width), col(0)),
            pl.BlockSpec((tb, width), col(0)),
            pl.BlockSpec((tb, width), col(C_HGI // width)),
        ],
        out_specs=[
            pl.BlockSpec((tb, width), col(0)),
            pl.BlockSpec((HG_HEADS, HG_DIM, HG_DIM), lambda b, t: (b, 0, 0)),
        ],
        out_shape=[
            jax.ShapeDtypeStruct((P_ROWS, width), F32),
            jax.ShapeDtypeStruct((N_BATCH * HG_HEADS, HG_DIM, HG_DIM), F32),
        ],
        scratch_shapes=[pltpu.VMEM((HG_HEADS, HG_DIM, HG_DIM), F32), pltpu.VMEM((HG_HEADS, HG_CHUNK, HG_DIM), F32)],
        compiler_params=_cp(("parallel", "arbitrary")),
        name="hgrn_prompt",
    )(y, logf, kf, y)


def _hgrn_sample_kernel(q_ref, g_ref, kf_ref, i_ref, s_ref, o_ref, so_ref, lhs_sc, kd_sc, i_sc, *, dbb):
    T = DEC_SEQ
    for d in range(dbb):
        for h in range(HG_HEADS):
            cs = slice(h * HG_DIM, (h + 1) * HG_DIM)
            rows = [d * T + t for t in range(T)]
            q = [q_ref[r:r + 1, cs] * HG_SCALE for r in rows]
            g = [g_ref[r:r + 1, cs] for r in rows]
            k = [kf_ref[r:r + 1, cs] for r in rows]
            iv = [i_ref[r:r + 1, cs] for r in rows]
            G = [g[0]]
            for t in range(1, T):
                G.append(G[t - 1] + g[t])
            st = s_ref[d, h].T
            lhs_sc[...] = jnp.zeros(lhs_sc.shape, F32)
            kd_sc[...] = jnp.zeros(kd_sc.shape, F32)
            i_sc[...] = jnp.zeros(i_sc.shape, F32)
            for t in range(T):
                lhs_sc[t:t + 1, :] = q[t] * jnp.exp(G[t])
                kd_sc[t:t + 1, :] = k[t] * jnp.exp(G[T - 1] - G[t])
                i_sc[t:t + 1, :] = iv[t]
            o = _dot_nt(lhs_sc[...].astype(BF16), st.astype(BF16))
            for t in range(T):
                ot = o[t:t + 1]
                for s in range(t + 1):
                    a = jnp.sum(q[t] * k[s] * jnp.exp(G[t] - G[s]), axis=-1, keepdims=True)
                    ot = ot + a * iv[s]
                o_ref[rows[t]:rows[t] + 1, cs] = ot
            st_new = jnp.exp(G[T - 1]) * st + _dot_tn(i_sc[...].astype(BF16), kd_sc[...].astype(BF16))
            so_ref[d, h] = st_new.T


def hgrn_sample(y, logf, kf, state, layer, dbb=2):
    rb = dbb * DEC_SEQ
    r0 = P_ROWS // rb
    kern = functools.partial(_hgrn_sample_kernel, dbb=dbb)
    row = lambda c: (lambda i: (r0 + i, c))
    return pl.pallas_call(
        kern,
        grid=(DEC_BATCH // dbb,),
        in_specs=[
            pl.BlockSpec((rb, 512), row(C_HGQ // 512)),
            pl.BlockSpec((rb, 512), row(0)),
            pl.BlockSpec((rb, 512), row(0)),
            pl.BlockSpec((rb, 512), row(C_HGI // 512)),
            pl.BlockSpec((None, dbb, HG_HEADS, HG_DIM, HG_DIM), lambda i: (layer, i, 0, 0, 0)),
        ],
        out_specs=[
            pl.BlockSpec((rb, 512), lambda i: (i, 0)),
            pl.BlockSpec((dbb, HG_HEADS, HG_DIM, HG_DIM), lambda i: (i, 0, 0, 0)),
        ],
        out_shape=[
            jax.ShapeDtypeStruct((S_ROWS, 512), F32),
            jax.ShapeDtypeStruct((DEC_BATCH, HG_HEADS, HG_DIM, HG_DIM), F32),
        ],
        scratch_shapes=[pltpu.VMEM((8, HG_DIM), F32), pltpu.VMEM((8, HG_DIM), F32), pltpu.VMEM((8, HG_DIM), F32)],
        compiler_params=_cp(("parallel",)),
        name="hgrn_sample",
    )(y, logf, kf, y, state)


def _headmm_kernel(a_ref, w_ref, o_ref):
    o_ref[...] = _dot(a_ref[...], w_ref[0]).astype(o_ref.dtype)


def q_latent(q_full, w_ukT):
    return pl.pallas_call(
        _headmm_kernel,
        grid=(MLA_HEADS,),
        in_specs=[pl.BlockSpec((S_ROWS, 128), lambda h: (P_ROWS // S_ROWS, 2 * h)),
                  pl.BlockSpec((1, 128, 256), lambda h: (h, 0, 0))],
        out_specs=pl.BlockSpec((S_ROWS, 256), lambda h: (0, h)),
        out_shape=jax.ShapeDtypeStruct((S_ROWS, MLA_HEADS * 256), BF16),
        compiler_params=_cp(("parallel",)),
        name="q_latent",
    )(q_full, w_ukT)


def o_latent_up(o_lat, w_uv3):
    return pl.pallas_call(
        _headmm_kernel,
        grid=(MLA_HEADS,),
        in_specs=[pl.BlockSpec((S_ROWS, 256), lambda h: (0, h)),
                  pl.BlockSpec((1, 256, 128), lambda h: (h, 0, 0))],
        out_specs=pl.BlockSpec((S_ROWS, 128), lambda h: (0, h)),
        out_shape=jax.ShapeDtypeStruct((S_ROWS, MLA_HEADS * MLA_V), BF16),
        compiler_params=_cp(("parallel",)),
        name="o_latent_up",
    )(o_lat, w_uv3)


def _mla_sample_kernel(pt_ref, ql_ref, qf_ref, cn_ref, rn_ref, cc_hbm, cr_hbm, o_ref,
                       cbuf, rbuf, csem, rsem, m_sc, l_sc, acc_sc, *, layer):
    b = pl.program_id(0)
    g = pl.program_id(1)
    ng = pl.num_programs(1)
    n = b * ng + g
    slot = lax.rem(n, 2)

    def group_copies(bb, gg, sl):
        cps = []
        for p in range(MLA_GROUP):
            page = pt_ref[bb * N_PAGES + gg * MLA_GROUP + p]
            cps.append(pltpu.make_async_copy(
                cc_hbm.at[layer, page], cbuf.at[sl, pl.ds(p * PAGE, PAGE)], csem.at[sl]))
            cps.append(pltpu.make_async_copy(cr_hbm.at[layer, page], rbuf.at[sl, p], rsem.at[sl]))
        return cps

    @pl.when(n == 0)
    def _():
        for cp in group_copies(0, 0, 0):
            cp.start()

    @pl.when(n + 1 < pl.num_programs(0) * ng)
    def _():
        for cp in group_copies((n + 1) // ng, lax.rem(n + 1, ng), 1 - slot):
            cp.start()

    ql = ql_ref[0]
    qr = qf_ref[0][:, 128:128 + MLA_ROPE]

    @pl.when(g == 0)
    def _():
        qlf = ql.astype(F32)
        qrf = qr.astype(F32)
        t_row = lax.broadcasted_iota(jnp.int32, (DEC_SEQ * MLA_HEADS, 1), 0) // MLA_HEADS
        cs, ss = [], []
        for s in range(DEC_SEQ):
            c = cn_ref[0, s:s + 1, :]
            r = rn_ref[0, s:s + 1, 0:MLA_ROPE]
            sc = (jnp.sum(qlf * c, axis=-1, keepdims=True) + jnp.sum(qrf * r, axis=-1, keepdims=True)) * MLA_SCALE
            ss.append(jnp.where(t_row >= s, sc, -jnp.inf))
            cs.append(c)
        m = ss[0]
        for s in range(1, DEC_SEQ):
            m = jnp.maximum(m, ss[s])
        l = jnp.zeros_like(m)
        acc = jnp.zeros(acc_sc.shape, F32)
        for s in range(DEC_SEQ):
            p = jnp.exp(ss[s] - m)
            l = l + p
            acc = acc + p * cs[s]
        m_sc[...] = jnp.broadcast_to(m, m_sc.shape)
        l_sc[...] = jnp.broadcast_to(l, l_sc.shape)
        acc_sc[...] = acc

    for cp in group_copies(b, g, slot):
        cp.wait()

    keys = MLA_SUB * PAGE
    m_new = m_sc[...]
    l_new = l_sc[...]
    acc_new = acc_sc[...]
    for blk in range(MLA_GROUP // MLA_SUB):
        cb = cbuf[slot, blk * keys:(blk + 1) * keys, :].astype(BF16)
        s_rope = jnp.concatenate(
            [_dot(qr, rbuf[slot, blk * MLA_SUB + p].astype(BF16)) for p in range(MLA_SUB)], axis=1)
        s = (_dot_nt(ql, cb) + s_rope) * MLA_SCALE
        m_prev = m_new
        m_new = jnp.maximum(m_prev, jnp.max(s, axis=-1, keepdims=True))
        alpha = jnp.exp(m_prev - m_new)
        pr = jnp.exp(s - _tile_lanes(m_new, keys // 128))
        l_new = alpha * l_new + jnp.sum(pr, axis=-1, keepdims=True)
        acc_new = _tile_lanes(alpha, 2) * acc_new + _dot(pr.astype(BF16), cb)
    m_sc[...] = m_new
    l_sc[...] = l_new
    acc_sc[...] = acc_new

    @pl.when(g == ng - 1)
    def _():
        o_ref[0] = (acc_new / _tile_lanes(l_new, 2)).astype(o_ref.dtype)


def mla_sample(pt_flat, q_lat3, q_full3, c_new, r_new, cache_ckv, cache_krT, layer):
    ng = N_PAGES // MLA_GROUP
    rows = DEC_SEQ * MLA_HEADS
    kern = functools.partial(_mla_sample_kernel, layer=layer)
    per_b = lambda b, g, pt: (b, 0, 0)
    return pl.pallas_call(
        kern,
        grid_spec=pltpu.PrefetchScalarGridSpec(
            num_scalar_prefetch=1,
            grid=(DEC_BATCH, ng),
            in_specs=[
                pl.BlockSpec((1, rows, 256), per_b),
                pl.BlockSpec((1, rows, 256), per_b),
                pl.BlockSpec((1, DEC_SEQ, 256), per_b),
                pl.BlockSpec((1, DEC_SEQ, 128), per_b),
                pl.BlockSpec(memory_space=pl.ANY),
                pl.BlockSpec(memory_space=pl.ANY),
            ],
            out_specs=pl.BlockSpec((1, rows, 256), per_b),
            scratch_shapes=[
                pltpu.VMEM((2, MLA_GROUP * PAGE, MLA_KV_LORA), F32),
                pltpu.VMEM((2, MLA_GROUP, MLA_ROPE, PAGE), F32),
                pltpu.SemaphoreType.DMA((2,)),
                pltpu.SemaphoreType.DMA((2,)),
                pltpu.VMEM((rows, 128), F32), pltpu.VMEM((rows, 128), F32), pltpu.VMEM((rows, 256), F32),
            ],
        ),
        out_shape=jax.ShapeDtypeStruct((DEC_BATCH, rows, 256), BF16),
        compiler_params=_cp(("arbitrary", "arbitrary")),
        name="mla_sample",
    )(pt_flat, q_lat3, q_full3, c_new, r_new, cache_ckv, cache_krT)


def _sb_sample_kernel(pt_ref, q_ref, kn_ref, vn_ref, later_ref, ck_hbm, cv_hbm, o_ref,
                      kbuf, vbuf, ksem, vsem, *, layer):
    b = pl.program_id(0)
    slot = lax.rem(b, 2)
    rows = DEC_SEQ * SB_HEADS
    n_groups = N_PAGES // SB_GROUP

    def group_copies(bb, gg, sl):
        cps = []
        for p in range(SB_GROUP):
            page = pt_ref[bb * N_PAGES + (N_PAGES - 1) - (gg * SB_GROUP + p)]
            dst = pl.ds((SB_GROUP - 1 - p) * PAGE, PAGE)
            cps.append(pltpu.make_async_copy(ck_hbm.at[layer, page], kbuf.at[sl, dst], ksem.at[sl]))
            cps.append(pltpu.make_async_copy(cv_hbm.at[layer, page], vbuf.at[sl, dst], vsem.at[sl]))
        return cps

    @pl.when(b == 0)
    def _():
        for cp in group_copies(0, 0, 0):
            cp.start()

    @pl.when(b + 1 < pl.num_programs(0))
    def _():
        for cp in group_copies(b + 1, 0, 1 - slot):
            cp.start()

    qf = q_ref[0]
    qb = qf.astype(BF16)

    t_row = lax.broadcasted_iota(jnp.int32, (rows, 1), 0) // SB_HEADS
    ls, neg, valid = [], [], []
    for s in range(DEC_SEQ):
        z = jnp.sum(qf * kn_ref[0, s:s + 1, :], axis=-1, keepdims=True) * SB_SCALE
        t = _softplus_neg_abs(z)
        ok = t_row > s
        valid.append(ok)
        ls.append(jnp.minimum(z, 0.0) - t)
        neg.append(jnp.where(ok, jnp.minimum(-z, 0.0) - t, 0.0))
    acc = jnp.zeros((rows, SB_DIM), F32)
    carry = jnp.zeros((rows, 1), F32)
    for s in range(DEC_SEQ - 1, -1, -1):
        a = jnp.where(valid[s], jnp.exp(ls[s] + carry), 0.0)
        acc = acc + a * vn_ref[0, s:s + 1, :]
        carry = carry + neg[s]

    def process(sl, carry, acc):
        k = kbuf[sl].astype(BF16)
        v = vbuf[sl].astype(BF16)
        z = _dot_nt(qb, k) * SB_SCALE
        t = _softplus_neg_abs(z)
        ls = jnp.minimum(z, 0.0) - t
        neg = jnp.minimum(-z, 0.0) - t
        hi, lo = _split2(neg)
        later = later_ref[...]
        after = _dot(hi, later) + _dot(lo, later) + carry
        a = jnp.exp(ls + after)
        return carry + jnp.sum(neg, axis=-1, keepdims=True), acc + _dot(a.astype(BF16), v)

    for cp in group_copies(b, 0, slot):
        cp.wait()
    carry, acc = process(slot, carry, acc)

    def more(st):
        return jnp.logical_and(st[0] < n_groups, jnp.max(st[1]) >= SB_DONE)

    def fetch_and_process(st):
        gg, carry, acc = st
        cps = group_copies(b, gg, 2)
        for cp in cps:
            cp.start()
        for cp in cps:
            cp.wait()
        carry, acc = process(2, carry, acc)
        return gg + 1, carry, acc

    _, _, acc = lax.while_loop(more, fetch_and_process, (jnp.int32(1), carry, acc))
    o_ref[0] = acc.astype(o_ref.dtype)


def sb_sample(pt_flat, q3, k_new, v_new, cache_k, cache_v, layer):
    rows = DEC_SEQ * SB_HEADS
    keys = SB_GROUP * PAGE
    kern = functools.partial(_sb_sample_kernel, layer=layer)
    later = (jnp.arange(keys)[:, None] > jnp.arange(keys)[None, :]).astype(BF16)
    per_b = lambda b, pt: (b, 0, 0)
    return pl.pallas_call(
        kern,
        grid_spec=pltpu.PrefetchScalarGridSpec(
            num_scalar_prefetch=1,
            grid=(DEC_BATCH,),
            in_specs=[
                pl.BlockSpec((1, rows, SB_DIM), per_b),
                pl.BlockSpec((1, DEC_SEQ, SB_DIM), per_b),
                pl.BlockSpec((1, DEC_SEQ, SB_DIM), per_b),
                pl.BlockSpec((keys, keys), lambda b, pt: (0, 0)),
                pl.BlockSpec(memory_space=pl.ANY),
                pl.BlockSpec(memory_space=pl.ANY),
            ],
            out_specs=pl.BlockSpec((1, rows, SB_DIM), per_b),
            scratch_shapes=[
                pltpu.VMEM((3, keys, SB_DIM), F32),
                pltpu.VMEM((3, keys, SB_DIM), F32),
                pltpu.SemaphoreType.DMA((3,)),
                pltpu.SemaphoreType.DMA((3,)),
            ],
        ),
        out_shape=jax.ShapeDtypeStruct((DEC_BATCH, rows, SB_DIM), BF16),
        compiler_params=_cp(("arbitrary",)),
        name="sb_sample",
    )(pt_flat, q3, k_new, v_new, later, cache_k, cache_v)


def _merge_kernel(mla_p_ref, sb_p_ref, hgo_p_ref, mla_s_ref, sb_s_ref, hgo_s_ref, hgg_ref, gm_ref, gs_ref, gh_ref,
                  hn_ref, lm_ref, ls_ref, lh_ref, o_ref, mla_sc, sb_sc, hg_sc):
    is_sample = pl.program_id(0) == pl.num_programs(0) - 1
    first = pl.program_id(1) == 0

    def stage(mla_ref, sb_ref, hgo_ref):
        mla_sc[...] = mla_ref[...]
        sb_sc[...] = sb_ref[...]
        gate = hgg_ref[...]
        act = gate * _sigmoid(gate)
        for h in range(HG_HEADS):
            cs = slice(h * HG_DIM, (h + 1) * HG_DIM)
            hg_sc[:, cs] = (_rms(hgo_ref[:, cs], hn_ref[...]) * act[:, cs]).astype(BF16)

    @pl.when(jnp.logical_and(first, jnp.logical_not(is_sample)))
    def _():
        stage(mla_p_ref, sb_p_ref, hgo_p_ref)

    @pl.when(jnp.logical_and(first, is_sample))
    def _():
        stage(mla_s_ref, sb_s_ref, hgo_s_ref)

    m = (_sigmoid(gm_ref[...]) * _dot(mla_sc[...], lm_ref[...])
         + _sigmoid(gs_ref[...]) * _dot(sb_sc[...], ls_ref[...])
         + _sigmoid(gh_ref[...]) * _dot(hg_sc[...], lh_ref[...]))
    o_ref[...] = m.astype(o_ref.dtype)


def merge(mla_p, sb_p, hg_p, mla_s, sb_s, hg_s, y, hg_norm, lift_mla, lift_sb, lift_hg, layer, bn=512):
    bm = S_ROWS
    n_prompt = P_ROWS // bm
    prow = lambda i, j: (jnp.minimum(i, n_prompt - 1), 0)
    srow = lambda i, j: (0, 0)
    row = lambda c: (lambda i, j: (i, c))
    gate = lambda c0: (lambda i, j: (i, c0 // bn + j))
    wcol = lambda i, j: (layer, 0, j)
    return pl.pallas_call(
        _merge_kernel,
        grid=(M_ROWS // bm, D_MODEL // bn),
        in_specs=[
            pl.BlockSpec((bm, 1024), prow),
            pl.BlockSpec((bm, 512), prow),
            pl.BlockSpec((bm, 512), prow),
            pl.BlockSpec((bm, 1024), srow),
            pl.BlockSpec((bm, 512), srow),
            pl.BlockSpec((bm, 512), srow),
            pl.BlockSpec((bm, 512), row(C_HGG // 512)),
            pl.BlockSpec((bm, bn), gate(C_GMLA)),
            pl.BlockSpec((bm, bn), gate(C_GSB)),
            pl.BlockSpec((bm, bn), gate(C_GHG)),
            pl.BlockSpec((1, HG_DIM), lambda i, j: (0, 0)),
            pl.BlockSpec((None, 1024, bn), wcol),
            pl.BlockSpec((None, 512, bn), wcol),
            pl.BlockSpec((None, 512, bn), wcol),
        ],
        out_specs=pl.BlockSpec((bm, bn), lambda i, j: (i, j)),
        out_shape=jax.ShapeDtypeStruct((M_ROWS, D_MODEL), BF16),
        scratch_shapes=[pltpu.VMEM((bm, 1024), BF16), pltpu.VMEM((bm, 512), BF16), pltpu.VMEM((bm, 512), BF16)],
        compiler_params=_cp(("parallel", "arbitrary")),
        name="merge",
    )(mla_p, sb_p, hg_p, mla_s, sb_s, hg_s, y, y, y, y, hg_norm.reshape(1, -1), lift_mla, lift_sb, lift_hg)


def _ffn_gu_kernel(x_ref, gn_ref, wg_ref, wu_ref, o_ref, h_sc):
    @pl.when(pl.program_id(1) == 0)
    def _():
        h_sc[...] = _rms(x_ref[...], gn_ref[...]).astype(BF16)

    h = h_sc[...]
    g = _dot(h, wg_ref[...])
    u = _dot(h, wu_ref[...])
    o_ref[...] = (g * _sigmoid(g) * u).astype(o_ref.dtype)


def ffn_gate_up(x, gn, wg, wu, layer, bm=BM, bn=512):
    return pl.pallas_call(
        _ffn_gu_kernel,
        grid=(M_ROWS // bm, D_FF // bn),
        in_specs=[pl.BlockSpec((bm, D_MODEL), lambda i, j: (i, 0)),
                  pl.BlockSpec((1, D_MODEL), lambda i, j: (0, 0)),
                  pl.BlockSpec((None, D_MODEL, bn), lambda i, j: (layer, 0, j)),
                  pl.BlockSpec((None, D_MODEL, bn), lambda i, j: (layer, 0, j))],
        out_specs=pl.BlockSpec((bm, bn), lambda i, j: (i, j)),
        out_shape=jax.ShapeDtypeStruct((M_ROWS, D_FF), BF16),
        scratch_shapes=[pltpu.VMEM((bm, D_MODEL), BF16)],
        compiler_params=_cp(("parallel", "arbitrary")),
        name="ffn_gate_up",
    )(x, gn.reshape(1, -1), wg, wu)


def _rope_tables():
    half = MLA_ROPE // 2
    inv = ROPE_THETA ** (-jnp.arange(half, dtype=F32) / half)
    pos = jnp.concatenate([
        jnp.tile(jnp.arange(SEQ, dtype=jnp.int32), N_BATCH),
        N_PAGES * PAGE + jnp.tile(jnp.arange(DEC_SEQ, dtype=jnp.int32), DEC_BATCH),
    ]).astype(F32)
    ang = pos[:, None] * inv[None, :]
    pad = jnp.zeros((M_ROWS, 128 - MLA_ROPE), F32)
    cos = jnp.concatenate([jnp.cos(ang), jnp.cos(ang), pad], axis=-1)
    sin = jnp.concatenate([jnp.sin(ang), jnp.sin(ang), pad], axis=-1)
    return cos, sin


def _permute_w_in(w):
    pad = jnp.zeros((w.shape[0], N_IN - 9792), w.dtype)
    return jnp.concatenate([w[:, 0:512], w[:, 832:1344], w[:, 512:768], w[:, 1344:9792], w[:, 768:832], pad],
                           axis=-1).astype(BF16)


def kernel(x_prompt, x_sample, cache_mla_ckv, cache_mla_krope, cache_sb_k, cache_sb_v, state_hgrn, page_table, norm_attn, w_in, mla_q_norm, mla_w_uq, mla_kv_norm, mla_w_uk, mla_w_uv, hg_lb_logits, hg_norm, w_lift_mla, w_lift_sb, w_lift_hg, w_out, norm_ffn, ffn_w_gate, ffn_w_up, ffn_w_down, norm_final):
    cos, sin = _rope_tables()
    lb_all = lower_bounds(hg_lb_logits.astype(F32))
    pt_flat = page_table.reshape(-1).astype(jnp.int32)
    cache_krT = jnp.swapaxes(cache_mla_krope, 2, 3)
    lift_mla_b, lift_sb_b, lift_hg_b = (w.astype(BF16) for w in (w_lift_mla, w_lift_sb, w_lift_hg))
    w_out_b, w_gate_b, w_up_b, w_down_b = (w.astype(BF16) for w in (w_out, ffn_w_gate, ffn_w_up, ffn_w_down))
    x = jnp.concatenate([x_prompt.reshape(P_ROWS, D_MODEL), x_sample.reshape(S_ROWS, D_MODEL)], axis=0)

    outs = {k: [] for k in ("p_ckv", "p_kr", "p_sk", "p_sv", "p_hs", "s_ckv", "s_kr", "s_sk", "s_sv", "s_hs")}
    for l in range(DEPTH):
        w_in_p = _permute_w_in(w_in[l])
        w_uq = jnp.pad(mla_w_uq[l].reshape(MLA_Q_LORA, MLA_HEADS, MLA_NOPE + MLA_ROPE),
                       ((0, 0), (0, 0), (0, MLA_QK - MLA_NOPE - MLA_ROPE))).reshape(MLA_Q_LORA, -1).astype(BF16)
        w_uk2 = mla_w_uk[l].reshape(MLA_KV_LORA, -1).astype(BF16)
        w_uv2 = mla_w_uv[l].reshape(MLA_KV_LORA, -1).astype(BF16)
        w_ukT = jnp.transpose(mla_w_uk[l], (1, 2, 0)).astype(BF16)
        w_uv3 = jnp.transpose(mla_w_uv[l], (1, 0, 2)).astype(BF16)

        y = norm_matmul(x, norm_attn[l], w_in_p, 768, F32, name="in_proj")
        q_full, ckv, kr, logf, kf = prep(y, cos, sin, mla_q_norm[l], w_uq, mla_kv_norm[l], lb_all[l])

        k_full, v = kv_up(ckv, kr, w_uk2, w_uv2)
        mla_p = mla_prompt(q_full, k_full, v)
        sb_p = sb_prompt(y)
        hg_p, hs_p = hgrn_prompt(y, logf, kf)

        q_lat = q_latent(q_full, w_ukT)
        o_lat = mla_sample(
            pt_flat,
            q_lat.reshape(DEC_BATCH, DEC_SEQ * MLA_HEADS, 256),
            q_full[P_ROWS:].reshape(DEC_BATCH, DEC_SEQ * MLA_HEADS, MLA_QK),
            ckv[P_ROWS:].reshape(DEC_BATCH, DEC_SEQ, MLA_KV_LORA),
            kr[P_ROWS:].reshape(DEC_BATCH, DEC_SEQ, 128),
            cache_mla_ckv, cache_krT, l)
        mla_s = o_latent_up(o_lat.reshape(S_ROWS, MLA_HEADS * 256), w_uv3)
        sb_s = sb_sample(
            pt_flat,
            y[P_ROWS:, C_SBQ:C_SBQ + 512].reshape(DEC_BATCH, DEC_SEQ * SB_HEADS, SB_DIM),
            y[P_ROWS:, C_SBK:C_SBK + 128].reshape(DEC_BATCH, DEC_SEQ, SB_DIM),
            y[P_ROWS:, C_SBV:C_SBV + 128].reshape(DEC_BATCH, DEC_SEQ, SB_DIM),
            cache_sb_k, cache_sb_v, l)
        hg_s, hs_s = hgrn_sample(y, logf, kf, state_hgrn, l)

        m = merge(mla_p, sb_p, hg_p, mla_s, sb_s.reshape(S_ROWS, SB_HEADS * SB_DIM), hg_s, y, hg_norm[l],
                  lift_mla_b, lift_sb_b, lift_hg_b, l)
        x = matmul(m, w_out_b, l, 512, F32, res=x, name="out_proj")
        act = ffn_gate_up(x, norm_ffn[l], w_gate_b, w_up_b, l)
        x = matmul(act, w_down_b, l, 512, F32, res=x, vmem=VMEM_LIMIT_BIG, name="ffn_down")

        outs["p_ckv"].append(ckv[:P_ROWS].reshape(N_BATCH, SEQ, MLA_KV_LORA))
        outs["p_kr"].append(kr[:P_ROWS, :MLA_ROPE].reshape(N_BATCH, SEQ, MLA_ROPE))
        outs["p_sk"].append(y[:P_ROWS, C_SBK:C_SBK + 128].reshape(N_BATCH, SEQ, SB_DIM))
        outs["p_sv"].append(y[:P_ROWS, C_SBV:C_SBV + 128].reshape(N_BATCH, SEQ, SB_DIM))
        outs["p_hs"].append(hs_p.reshape(N_BATCH, HG_HEADS, HG_DIM, HG_DIM))
        outs["s_ckv"].append(ckv[P_ROWS:].reshape(DEC_BATCH, DEC_SEQ, MLA_KV_LORA))
        outs["s_kr"].append(kr[P_ROWS:, :MLA_ROPE].reshape(DEC_BATCH, DEC_SEQ, MLA_ROPE))
        outs["s_sk"].append(y[P_ROWS:, C_SBK:C_SBK + 128].reshape(DEC_BATCH, DEC_SEQ, SB_DIM))
        outs["s_sv"].append(y[P_ROWS:, C_SBV:C_SBV + 128].reshape(DEC_BATCH, DEC_SEQ, SB_DIM))
        outs["s_hs"].append(hs_s)

    yf = rmsnorm(x, norm_final, F32)
    y_prompt = yf[:P_ROWS].reshape(N_BATCH, SEQ, D_MODEL)
    y_sample = yf[P_ROWS:].reshape(DEC_BATCH, DEC_SEQ, D_MODEL)
    st = lambda k: jnp.stack(outs[k])
    return (y_prompt, y_sample, st("p_ckv"), st("p_kr"), st("p_sk"), st("p_sv"), st("p_hs"),
            st("s_ckv"), st("s_kr"), st("s_sk"), st("s_sv"), st("s_hs"))
```

```python
import functools

import jax
import jax.numpy as jnp
from jax import lax
from jax.experimental import pallas as pl
from jax.experimental.pallas import tpu as pltpu

F32 = jnp.float32
BF16 = jnp.bfloat16

D_MODEL = 2048
DEPTH = 4
N_BATCH = 2
SEQ = 4096
DEC_BATCH = 128
DEC_SEQ = 4
PAGE = 128
N_PAGES = 64
P_ROWS = N_BATCH * SEQ
S_ROWS = DEC_BATCH * DEC_SEQ
M_ROWS = P_ROWS + S_ROWS

MLA_HEADS = 8
MLA_NOPE = 128
MLA_ROPE = 64
MLA_V = 128
MLA_Q_LORA = 512
MLA_KV_LORA = 256
MLA_QK = 256
ROPE_THETA = 10000.0
SB_HEADS = 4
SB_DIM = 128
HG_HEADS = 4
HG_DIM = 128
HG_CHUNK = 64
D_FF = 5632
RMS_EPS = 1e-6

C_CQ, C_SBQ, C_CKV, C_SBK, C_SBV = 0, 512, 1024, 1280, 1408
C_HGQ, C_HGF, C_HGI, C_HGG = 1536, 2048, 2560, 3072
C_GMLA, C_GSB, C_GHG, C_KPE = 3584, 5632, 7680, 9728
N_IN = 9984

BM = 1088
VMEM_LIMIT = 48 * 1024 * 1024
VMEM_LIMIT_BIG = 56 * 1024 * 1024

SB_GROUP = 4
MLA_GROUP = 32
MLA_SUB = 8
SB_DONE = -104.0

MLA_SCALE = (MLA_NOPE + MLA_ROPE) ** -0.5
MLA_SCALE_LOG2E = MLA_SCALE * 1.4426950408889634
SB_SCALE = SB_DIM ** -0.5
HG_SCALE = HG_DIM ** -0.5


def _cp(sem, vmem=VMEM_LIMIT):
    return pltpu.CompilerParams(dimension_semantics=sem, vmem_limit_bytes=vmem)


def _dot(a, b):
    return jnp.dot(a, b, preferred_element_type=F32)


def _dot_nt(a, b):
    return lax.dot_general(a, b, (((1,), (1,)), ((), ())), preferred_element_type=F32)


def _dot_tn(a, b):
    return lax.dot_general(a, b, (((0,), (0,)), ((), ())), preferred_element_type=F32)


def _tile_lanes(x, n):
    return jnp.tile(x, (1, n))


def _sigmoid(x):
    return 0.5 * jnp.tanh(0.5 * x) + 0.5


def _rms(x, g):
    return x * lax.rsqrt(jnp.mean(x * x, axis=-1, keepdims=True) + RMS_EPS) * g


def _softplus_neg_abs(z):
    return jnp.log1p(jnp.exp(-jnp.abs(z)))


def _split2(x):
    hi = x.astype(BF16)
    lo = (x - hi.astype(F32)).astype(BF16)
    return hi, lo


def _split3(x):
    hi = x.astype(BF16)
    r = x - hi.astype(F32)
    mid = r.astype(BF16)
    lo = (r - mid.astype(F32)).astype(BF16)
    return hi, mid, lo


def _norm_kernel(x_ref, g_ref, o_ref):
    o_ref[...] = _rms(x_ref[...], g_ref[...]).astype(o_ref.dtype)


def rmsnorm(x, g, out_dtype, bm=544):
    m, d = x.shape
    return pl.pallas_call(
        _norm_kernel,
        grid=(m // bm,),
        in_specs=[pl.BlockSpec((bm, d), lambda i: (i, 0)), pl.BlockSpec((1, d), lambda i: (0, 0))],
        out_specs=pl.BlockSpec((bm, d), lambda i: (i, 0)),
        out_shape=jax.ShapeDtypeStruct((m, d), out_dtype),
        compiler_params=_cp(("parallel",)),
        name="rmsnorm",
    )(x, g.reshape(1, d))


def _mm_kernel(a_ref, w_ref, o_ref):
    o_ref[...] = _dot(a_ref[...], w_ref[...]).astype(o_ref.dtype)


def _mm_res_kernel(a_ref, w_ref, r_ref, o_ref):
    o_ref[...] = r_ref[...] + _dot(a_ref[...], w_ref[...])


def _norm_mm_kernel(x_ref, g_ref, w_ref, o_ref, h_sc):
    @pl.when(pl.program_id(1) == 0)
    def _():
        h_sc[...] = _rms(x_ref[...], g_ref[...]).astype(BF16)

    o_ref[...] = _dot(h_sc[...], w_ref[...]).astype(o_ref.dtype)


def norm_matmul(x, g, w, bn, out_dtype, bm=BM, name="norm_matmul"):
    m, k = x.shape
    n = w.shape[1]
    return pl.pallas_call(
        _norm_mm_kernel,
        grid=(m // bm, n // bn),
        in_specs=[pl.BlockSpec((bm, k), lambda i, j: (i, 0)),
                  pl.BlockSpec((1, k), lambda i, j: (0, 0)),
                  pl.BlockSpec((k, bn), lambda i, j: (0, j))],
        out_specs=pl.BlockSpec((bm, bn), lambda i, j: (i, j)),
        out_shape=jax.ShapeDtypeStruct((m, n), out_dtype),
        scratch_shapes=[pltpu.VMEM((bm, k), BF16)],
        compiler_params=_cp(("parallel", "arbitrary"), VMEM_LIMIT_BIG),
        name=name,
    )(x, g.reshape(1, k), w)


def matmul(a, w, layer, bn, out_dtype, res=None, bm=BM, vmem=VMEM_LIMIT, name="matmul"):
    m, k = a.shape
    n = w.shape[2]
    in_specs = [pl.BlockSpec((bm, k), lambda i, j: (i, 0)), pl.BlockSpec((None, k, bn), lambda i, j: (layer, 0, j))]
    args = [a, w]
    body = _mm_kernel
    if res is not None:
        in_specs.append(pl.BlockSpec((bm, bn), lambda i, j: (i, j)))
        args.append(res)
        body = _mm_res_kernel
    return pl.pallas_call(
        body,
        grid=(m // bm, n // bn),
        in_specs=in_specs,
        out_specs=pl.BlockSpec((bm, bn), lambda i, j: (i, j)),
        out_shape=jax.ShapeDtypeStruct((m, n), out_dtype),
        compiler_params=_cp(("parallel", "arbitrary"), vmem),
        name=name,
    )(*args)


def _lb_kernel(x_ref, o_ref):
    x = x_ref[...]
    e = jnp.exp(x - jnp.max(x, axis=0, keepdims=True))
    p = e / jnp.sum(e, axis=0, keepdims=True)
    cum0 = p[0:1]
    cum = cum0
    o_ref[0:1, :] = cum - cum0
    for l in range(1, DEPTH):
        cum = cum + p[l:l + 1]
        o_ref[l:l + 1, :] = cum - cum0


def lower_bounds(logits):
    return pl.pallas_call(
        _lb_kernel, out_shape=jax.ShapeDtypeStruct(logits.shape, F32), name="hg_lower_bounds")(logits)


def _rope(x, cos, sin, first):
    rot = jnp.where(first, -pltpu.roll(x, 96, 1), pltpu.roll(x, 32, 1))
    return x * cos + rot * sin


def _prep_kernel(cq_ref, ckv_ref, kpe_ref, hf_ref, cos_ref, sin_ref, qn_ref, wuq_ref, kvn_ref, lb_ref,
                 q_ref, ckvo_ref, kro_ref, logf_ref, kf_ref):
    cos = cos_ref[...]
    sin = sin_ref[...]
    first = lax.broadcasted_iota(jnp.int32, cos.shape, 1) < (MLA_ROPE // 2)
    cqn = _rms(cq_ref[...], qn_ref[...]).astype(BF16)
    q = _dot(cqn, wuq_ref[...])
    for h in range(MLA_HEADS):
        a = h * MLA_QK
        q_ref[:, a:a + 128] = q[:, a:a + 128].astype(BF16)
        q_ref[:, a + 128:a + 256] = _rope(q[:, a + 128:a + 256], cos, sin, first).astype(BF16)
    ckvo_ref[...] = _rms(ckv_ref[...], kvn_ref[...])
    kro_ref[...] = _rope(kpe_ref[...], cos, sin, first)
    z = hf_ref[...]
    lb = lb_ref[...]
    t = _softplus_neg_abs(z)
    a_ = jnp.log(lb)
    b_ = jnp.log1p(-lb) + (jnp.minimum(z, 0.0) - t)
    logf_ref[...] = jnp.maximum(a_, b_) + jnp.log1p(jnp.exp(-jnp.abs(a_ - b_)))
    kf_ref[...] = (1.0 - lb) * jnp.exp(jnp.minimum(-z, 0.0) - t)


def prep(y, cos, sin, q_norm, w_uq, kv_norm, lb, bm=512):
    m = y.shape[0]
    row = lambda c: (lambda i: (i, c))
    const = lambda i: (0, 0)
    return pl.pallas_call(
        _prep_kernel,
        grid=(m // bm,),
        in_specs=[
            pl.BlockSpec((bm, 512), row(C_CQ // 512)),
            pl.BlockSpec((bm, 256), row(C_CKV // 256)),
            pl.BlockSpec((bm, 128), row(C_KPE // 128)),
            pl.BlockSpec((bm, 512), row(C_HGF // 512)),
            pl.BlockSpec((bm, 128), row(0)),
            pl.BlockSpec((bm, 128), row(0)),
            pl.BlockSpec((1, 512), const),
            pl.BlockSpec((MLA_Q_LORA, MLA_HEADS * MLA_QK), const),
            pl.BlockSpec((1, 256), const),
            pl.BlockSpec((1, 512), const),
        ],
        out_specs=[
            pl.BlockSpec((bm, MLA_HEADS * MLA_QK), row(0)),
            pl.BlockSpec((bm, 256), row(0)),
            pl.BlockSpec((bm, 128), row(0)),
            pl.BlockSpec((bm, 512), row(0)),
            pl.BlockSpec((bm, 512), row(0)),
        ],
        out_shape=[
            jax.ShapeDtypeStruct((m, MLA_HEADS * MLA_QK), BF16),
            jax.ShapeDtypeStruct((m, 256), F32),
            jax.ShapeDtypeStruct((m, 128), F32),
            jax.ShapeDtypeStruct((m, 512), F32),
            jax.ShapeDtypeStruct((m, 512), F32),
        ],
        compiler_params=_cp(("parallel",)),
        name="prep",
    )(y, y, y, y, cos, sin, q_norm.reshape(1, -1), w_uq, kv_norm.reshape(1, -1), lb.reshape(1, -1))


def _kvup_kernel(ckv_ref, kr_ref, wuk_ref, wuv_ref, k_ref, v_ref):
    c = ckv_ref[...].astype(BF16)
    kn = _dot(c, wuk_ref[...])
    v_ref[...] = _dot(c, wuv_ref[...]).astype(BF16)
    kr = kr_ref[...].astype(BF16)
    for h in range(MLA_HEADS):
        a = h * MLA_QK
        k_ref[:, a:a + 128] = kn[:, h * 128:(h + 1) * 128].astype(BF16)
        k_ref[:, a + 128:a + 256] = kr


def kv_up(ckv, kr, w_uk, w_uv, bm=512):
    const = lambda i: (0, 0)
    row = lambda i: (i, 0)
    return pl.pallas_call(
        _kvup_kernel,
        grid=(P_ROWS // bm,),
        in_specs=[pl.BlockSpec((bm, 256), row), pl.BlockSpec((bm, 128), row),
                  pl.BlockSpec((256, 1024), const), pl.BlockSpec((256, 1024), const)],
        out_specs=[pl.BlockSpec((bm, MLA_HEADS * MLA_QK), row), pl.BlockSpec((bm, 1024), row)],
        out_shape=[jax.ShapeDtypeStruct((P_ROWS, MLA_HEADS * MLA_QK), BF16),
                   jax.ShapeDtypeStruct((P_ROWS, 1024), BF16)],
        compiler_params=_cp(("parallel",)),
        name="kv_up",
    )(ckv, kr, w_uk, w_uv)


def _mla_flash_kernel(ti_ref, tj_ref, q_ref, k_ref, v_ref, o_ref, m_sc, l_sc, acc_sc, *, tq, tk):
    step_id = pl.program_id(1)
    i = ti_ref[step_id]
    j = tj_ref[step_id]
    j_last = (i * tq + tq - 1) // tk

    @pl.when(j == 0)
    def _():
        m_sc[...] = jnp.full(m_sc.shape, -jnp.inf, F32)
        l_sc[...] = jnp.zeros(l_sc.shape, F32)
        acc_sc[...] = jnp.zeros(acc_sc.shape, F32)

    def step(masked):
        if masked:
            row = i * tq + lax.broadcasted_iota(jnp.int32, (tq, tk), 0)
            col = j * tk + lax.broadcasted_iota(jnp.int32, (tq, tk), 1)
            keep = col <= row
        for h in range(MLA_HEADS):
            q = q_ref[:, h * MLA_QK:(h + 1) * MLA_QK]
            k = k_ref[:, h * MLA_QK:(h + 1) * MLA_QK]
            v = v_ref[:, h * MLA_V:(h + 1) * MLA_V]
            s = _dot_nt(q, k)
            if masked:
                s = jnp.where(keep, s, -jnp.inf)
            m_prev = m_sc[h]
            m_new = jnp.maximum(m_prev, jnp.max(s, axis=-1, keepdims=True))
            alpha = jnp.exp2((m_prev - m_new) * MLA_SCALE_LOG2E)
            p = jnp.exp2((s - _tile_lanes(m_new, tk // 128)) * MLA_SCALE_LOG2E)
            l_sc[h] = alpha * l_sc[h] + jnp.sum(p, axis=-1, keepdims=True)
            acc_sc[h] = alpha * acc_sc[h] + _dot(p.astype(BF16), v)
            m_sc[h] = m_new

    @pl.when(j < j_last)
    def _():
        step(False)

    @pl.when(j == j_last)
    def _():
        step(True)
        for h in range(MLA_HEADS):
            o_ref[:, h * MLA_V:(h + 1) * MLA_V] = (acc_sc[h] / l_sc[h]).astype(o_ref.dtype)


def mla_prompt(q_full, k_full, v, tq=512, tk=512):
    nq = SEQ // tq
    nk = SEQ // tk
    pairs = [(i, j) for i in range(nq) for j in range((i * tq + tq - 1) // tk + 1)]
    ti = jnp.asarray([p[0] for p in pairs], jnp.int32)
    tj = jnp.asarray([p[1] for p in pairs], jnp.int32)
    kern = functools.partial(_mla_flash_kernel, tq=tq, tk=tk)
    return pl.pallas_call(
        kern,
        grid_spec=pltpu.PrefetchScalarGridSpec(
            num_scalar_prefetch=2,
            grid=(N_BATCH, len(pairs)),
            in_specs=[
                pl.BlockSpec((tq, MLA_HEADS * MLA_QK), lambda b, s, ti, tj: (b * nq + ti[s], 0)),
                pl.BlockSpec((tk, MLA_HEADS * MLA_QK), lambda b, s, ti, tj: (b * nk + tj[s], 0)),
                pl.BlockSpec((tk, MLA_HEADS * MLA_V), lambda b, s, ti, tj: (b * nk + tj[s], 0)),
            ],
            out_specs=pl.BlockSpec((tq, MLA_HEADS * MLA_V), lambda b, s, ti, tj: (b * nq + ti[s], 0)),
            scratch_shapes=[pltpu.VMEM((MLA_HEADS, tq, 128), F32), pltpu.VMEM((MLA_HEADS, tq, 128), F32),
                            pltpu.VMEM((MLA_HEADS, tq, MLA_V), F32)],
        ),
        out_shape=jax.ShapeDtypeStruct((P_ROWS, MLA_HEADS * MLA_V), BF16),
        compiler_params=_cp(("parallel", "arbitrary")),
        name="mla_prompt",
    )(ti, tj, q_full, k_full, v)


def _sb_prompt_kernel(q_ref, k_ref, v_ref, o_ref, kb_sc, vb_sc, carry_sc, acc_sc, *, tq, tk):
    i = pl.program_id(1)
    ratio = tq // tk
    last = ratio * i + ratio - 1

    @pl.when(i == 0)
    def _():
        kb_sc[...] = k_ref[...].astype(BF16)
        vb_sc[...] = v_ref[...].astype(BF16)

    carry_sc[...] = jnp.zeros(carry_sc.shape, F32)
    acc_sc[...] = jnp.zeros(acc_sc.shape, F32)
    ur = lax.broadcasted_iota(jnp.int32, (tk, tk), 0)
    uc = lax.broadcasted_iota(jnp.int32, (tk, tk), 1)
    later = (ur > uc).astype(BF16)

    def step(jj, masked):
        rows = pl.ds(pl.multiple_of(jj * tk, tk), tk)
        k = kb_sc[rows, :]
        v = vb_sc[rows, :]
        if masked:
            row = i * tq + lax.broadcasted_iota(jnp.int32, (tq, tk), 0)
            col = jj * tk + lax.broadcasted_iota(jnp.int32, (tq, tk), 1)
            valid = col < row
        for h in range(SB_HEADS):
            q = q_ref[:, h * SB_DIM:(h + 1) * SB_DIM].astype(BF16)
            z = _dot_nt(q, k) * SB_SCALE
            t = _softplus_neg_abs(z)
            ls = jnp.minimum(z, 0.0) - t
            neg = jnp.minimum(-z, 0.0) - t
            if masked:
                neg = jnp.where(valid, neg, 0.0)
            hi, lo = _split2(neg)
            after = _dot(hi, later) + _dot(lo, later) + carry_sc[h]
            a = jnp.exp(ls + after)
            if masked:
                a = jnp.where(valid, a, 0.0)
            acc_sc[h] = acc_sc[h] + _dot(a.astype(BF16), v)
            carry_sc[h] = carry_sc[h] + jnp.sum(neg, axis=-1, keepdims=True)

    for m in range(ratio):
        step(last - m, True)

    def live():
        return (jnp.max(carry_sc[...]) >= SB_DONE).astype(jnp.int32)

    def more(st):
        return jnp.logical_and(st[0] >= 0, st[1] > 0)

    def older(st):
        step(st[0], False)
        return st[0] - 1, live()

    lax.while_loop(more, older, (ratio * i - 1, live()))

    for h in range(SB_HEADS):
        o_ref[:, h * SB_DIM:(h + 1) * SB_DIM] = acc_sc[h].astype(o_ref.dtype)


def sb_prompt(y, tq=512, tk=256):
    nq = SEQ // tq
    kern = functools.partial(_sb_prompt_kernel, tq=tq, tk=tk)
    return pl.pallas_call(
        kern,
        grid=(N_BATCH, nq),
        in_specs=[
            pl.BlockSpec((tq, SB_HEADS * SB_DIM), lambda b, i: (b * nq + i, C_SBQ // 512)),
            pl.BlockSpec((SEQ, SB_DIM), lambda b, i: (b, C_SBK // 128)),
            pl.BlockSpec((SEQ, SB_DIM), lambda b, i: (b, C_SBV // 128)),
        ],
        out_specs=pl.BlockSpec((tq, SB_HEADS * SB_DIM), lambda b, i: (b * nq + i, 0)),
        out_shape=jax.ShapeDtypeStruct((P_ROWS, SB_HEADS * SB_DIM), BF16),
        scratch_shapes=[pltpu.VMEM((SEQ, SB_DIM), BF16), pltpu.VMEM((SEQ, SB_DIM), BF16),
                        pltpu.VMEM((SB_HEADS, tq, 1), F32), pltpu.VMEM((SB_HEADS, tq, SB_DIM), F32)],
        compiler_params=_cp(("arbitrary", "arbitrary")),
        name="sb_prompt",
    )(y, y, y)


def _hgrn_prompt_kernel(q_ref, g_ref, kf_ref, i_ref, o_ref, so_ref, st_sc, dg_sc, *, nchunk):
    tstep = pl.program_id(1)
    C = HG_CHUNK
    SUB = 16

    @pl.when(tstep == 0)
    def _():
        st_sc[...] = jnp.zeros(st_sc.shape, F32)

    lr = lax.broadcasted_iota(jnp.int32, (C, C), 0)
    lc = lax.broadcasted_iota(jnp.int32, (C, C), 1)
    incl = (lc <= lr).astype(BF16)
    sr = lax.broadcasted_iota(jnp.int32, (SUB, HG_DIM), 0)

    def chunk_head(base, h):
        r = pl.ds(base, C)
        cs = slice(h * HG_DIM, (h + 1) * HG_DIM)
        g = g_ref[r, cs]
        qf = q_ref[r, cs] * HG_SCALE
        kf = kf_ref[r, cs]
        ii = i_ref[r, cs]
        iib = ii.astype(BF16)
        hi, mid, lo = _split3(g)
        G = _dot(incl, hi) + _dot(incl, mid) + _dot(incl, lo)
        st = st_sc[h]
        o = _dot_nt((qf * jnp.exp(G)).astype(BF16), st.astype(BF16))

        def offdiag(t0, t1, s0, s1):
            gm = G[s1 - 1:s1, :]
            qt = (qf[t0:t1] * jnp.exp(G[t0:t1] - gm)).astype(BF16)
            kt = (kf[s0:s1] * jnp.exp(gm - G[s0:s1])).astype(BF16)
            a = _dot_nt(qt, kt)
            return _dot(a.astype(BF16), iib[s0:s1])

        off_a = offdiag(32, 64, 0, 32)
        off_b0 = offdiag(16, 32, 0, 16)
        off_b1 = offdiag(48, 64, 32, 48)

        for blk in range(C // SUB):
            b0 = blk * SUB
            gb = G[b0:b0 + SUB]
            kb = kf[b0:b0 + SUB]
            ib = ii[b0:b0 + SUB]
            for t in range(SUB):
                keep = sr <= t
                dec = jnp.exp(jnp.where(keep, G[b0 + t:b0 + t + 1] - gb, 0.0))
                w = jnp.where(keep, qf[b0 + t:b0 + t + 1] * kb * dec, 0.0)
                a_col = jnp.sum(w, axis=-1, keepdims=True)
                dg_sc[h, b0 + t:b0 + t + 1, :] = jnp.sum(a_col * ib, axis=0, keepdims=True)

        dg = dg_sc[h]
        o_ref[pl.ds(base, 16), cs] = o[0:16] + dg[0:16]
        o_ref[pl.ds(base + 16, 16), cs] = o[16:32] + dg[16:32] + off_b0
        o_ref[pl.ds(base + 32, 16), cs] = o[32:48] + dg[32:48] + off_a[0:16]
        o_ref[pl.ds(base + 48, 16), cs] = o[48:64] + dg[48:64] + off_a[16:32] + off_b1

        g_last = G[C - 1:C, :]
        kd = (kf * jnp.exp(g_last - G)).astype(BF16)
        st_sc[h] = jnp.exp(g_last) * st + _dot_tn(iib, kd)

    def chunk(c, carry):
        base = pl.multiple_of(c * C, C)
        for h in range(HG_HEADS):
            chunk_head(base, h)
        return carry

    lax.fori_loop(0, nchunk, chunk, 0, unroll=2)

    @pl.when(tstep == pl.num_programs(1) - 1)
    def _():
        for h in range(HG_HEADS):
            so_ref[h] = st_sc[h].T


def hgrn_prompt(y, logf, kf, tb=512):
    nt = SEQ // tb
    width = HG_HEADS * HG_DIM
    kern = functools.partial(_hgrn_prompt_kernel, nchunk=tb // HG_CHUNK)
    col = lambda c0: (lambda b, t: (b * nt + t, c0))
    return pl.pallas_call(
        kern,
        grid=(N_BATCH, nt),
        in_specs=[
            pl.BlockSpec((tb, width), col(C_HGQ // width)),
            pl.BlockSpec((tb, width), col(0)),
            pl.BlockSpec((tb, width), col(0)),
            pl.BlockSpec((tb, width), col(C_HGI // width)),
        ],
        out_specs=[
            pl.BlockSpec((tb, width), col(0)),
            pl.BlockSpec((HG_HEADS, HG_DIM, HG_DIM), lambda b, t: (b, 0, 0)),
        ],
        out_shape=[
            jax.ShapeDtypeStruct((P_ROWS, width), F32),
            jax.ShapeDtypeStruct((N_BATCH * HG_HEADS, HG_DIM, HG_DIM), F32),
        ],
        scratch_shapes=[pltpu.VMEM((HG_HEADS, HG_DIM, HG_DIM), F32), pltpu.VMEM((HG_HEADS, HG_CHUNK, HG_DIM), F32)],
        compiler_params=_cp(("parallel", "arbitrary")),
        name="hgrn_prompt",
    )(y, logf, kf, y)


def _hgrn_sample_kernel(q_ref, g_ref, kf_ref, i_ref, s_ref, o_ref, so_ref, lhs_sc, kd_sc, i_sc, *, dbb):
    T = DEC_SEQ
    for d in range(dbb):
        for h in range(HG_HEADS):
            cs = slice(h * HG_DIM, (h + 1) * HG_DIM)
            rows = [d * T + t for t in range(T)]
            q = [q_ref[r:r + 1, cs] * HG_SCALE for r in rows]
            g = [g_ref[r:r + 1, cs] for r in rows]
            k = [kf_ref[r:r + 1, cs] for r in rows]
            iv = [i_ref[r:r + 1, cs] for r in rows]
            G = [g[0]]
            for t in range(1, T):
                G.append(G[t - 1] + g[t])
            st = s_ref[d, h].T
            lhs_sc[...] = jnp.zeros(lhs_sc.shape, F32)
            kd_sc[...] = jnp.zeros(kd_sc.shape, F32)
            i_sc[...] = jnp.zeros(i_sc.shape, F32)
            for t in range(T):
                lhs_sc[t:t + 1, :] = q[t] * jnp.exp(G[t])
                kd_sc[t:t + 1, :] = k[t] * jnp.exp(G[T - 1] - G[t])
                i_sc[t:t + 1, :] = iv[t]
            o = _dot_nt(lhs_sc[...].astype(BF16), st.astype(BF16))
            for t in range(T):
                ot = o[t:t + 1]
                for s in range(t + 1):
                    a = jnp.sum(q[t] * k[s] * jnp.exp(G[t] - G[s]), axis=-1, keepdims=True)
                    ot = ot + a * iv[s]
                o_ref[rows[t]:rows[t] + 1, cs] = ot
            st_new = jnp.exp(G[T - 1]) * st + _dot_tn(i_sc[...].astype(BF16), kd_sc[...].astype(BF16))
            so_ref[d, h] = st_new.T


def hgrn_sample(y, logf, kf, state, layer, dbb=2):
    rb = dbb * DEC_SEQ
    r0 = P_ROWS // rb
    kern = functools.partial(_hgrn_sample_kernel, dbb=dbb)
    row = lambda c: (lambda i: (r0 + i, c))
    return pl.pallas_call(
        kern,
        grid=(DEC_BATCH // dbb,),
        in_specs=[
            pl.BlockSpec((rb, 512), row(C_HGQ // 512)),
            pl.BlockSpec((rb, 512), row(0)),
            pl.BlockSpec((rb, 512), row(0)),
            pl.BlockSpec((rb, 512), row(C_HGI // 512)),
            pl.BlockSpec((None, dbb, HG_HEADS, HG_DIM, HG_DIM), lambda i: (layer, i, 0, 0, 0)),
        ],
        out_specs=[
            pl.BlockSpec((rb, 512), lambda i: (i, 0)),
            pl.BlockSpec((dbb, HG_HEADS, HG_DIM, HG_DIM), lambda i: (i, 0, 0, 0)),
        ],
        out_shape=[
            jax.ShapeDtypeStruct((S_ROWS, 512), F32),
            jax.ShapeDtypeStruct((DEC_BATCH, HG_HEADS, HG_DIM, HG_DIM), F32),
        ],
        scratch_shapes=[pltpu.VMEM((8, HG_DIM), F32), pltpu.VMEM((8, HG_DIM), F32), pltpu.VMEM((8, HG_DIM), F32)],
        compiler_params=_cp(("parallel",)),
        name="hgrn_sample",
    )(y, logf, kf, y, state)


def _headmm_kernel(a_ref, w_ref, o_ref):
    o_ref[...] = _dot(a_ref[...], w_ref[0]).astype(o_ref.dtype)


def q_latent(q_full, w_ukT):
    return pl.pallas_call(
        _headmm_kernel,
        grid=(MLA_HEADS,),
        in_specs=[pl.BlockSpec((S_ROWS, 128), lambda h: (P_ROWS // S_ROWS, 2 * h)),
                  pl.BlockSpec((1, 128, 256), lambda h: (h, 0, 0))],
        out_specs=pl.BlockSpec((S_ROWS, 256), lambda h: (0, h)),
        out_shape=jax.ShapeDtypeStruct((S_ROWS, MLA_HEADS * 256), BF16),
        compiler_params=_cp(("parallel",)),
        name="q_latent",
    )(q_full, w_ukT)


def o_latent_up(o_lat, w_uv3):
    return pl.pallas_call(
        _headmm_kernel,
        grid=(MLA_HEADS,),
        in_specs=[pl.BlockSpec((S_ROWS, 256), lambda h: (0, h)),
                  pl.BlockSpec((1, 256, 128), lambda h: (h, 0, 0))],
        out_specs=pl.BlockSpec((S_ROWS, 128), lambda h: (0, h)),
        out_shape=jax.ShapeDtypeStruct((S_ROWS, MLA_HEADS * MLA_V), BF16),
        compiler_params=_cp(("parallel",)),
        name="o_latent_up",
    )(o_lat, w_uv3)


def _mla_sample_kernel(pt_ref, ql_ref, qf_ref, cn_ref, rn_ref, cc_hbm, cr_hbm, o_ref,
                       cbuf, rbuf, csem, rsem, m_sc, l_sc, acc_sc, *, layer):
    b = pl.program_id(0)
    g = pl.program_id(1)
    ng = pl.num_programs(1)
    n = b * ng + g
    slot = lax.rem(n, 2)

    def group_copies(bb, gg, sl):
        cps = []
        for p in range(MLA_GROUP):
            page = pt_ref[bb * N_PAGES + gg * MLA_GROUP + p]
            cps.append(pltpu.make_async_copy(
                cc_hbm.at[layer, page], cbuf.at[sl, pl.ds(p * PAGE, PAGE)], csem.at[sl]))
            cps.append(pltpu.make_async_copy(cr_hbm.at[layer, page], rbuf.at[sl, p], rsem.at[sl]))
        return cps

    @pl.when(n == 0)
    def _():
        for cp in group_copies(0, 0, 0):
            cp.start()

    @pl.when(n + 1 < pl.num_programs(0) * ng)
    def _():
        for cp in group_copies((n + 1) // ng, lax.rem(n + 1, ng), 1 - slot):
            cp.start()

    ql = ql_ref[0]
    qr = qf_ref[0][:, 128:128 + MLA_ROPE]

    @pl.when(g == 0)
    def _():
        qlf = ql.astype(F32)
        qrf = qr.astype(F32)
        t_row = lax.broadcasted_iota(jnp.int32, (DEC_SEQ * MLA_HEADS, 1), 0) // MLA_HEADS
        cs, ss = [], []
        for s in range(DEC_SEQ):
            c = cn_ref[0, s:s + 1, :]
            r = rn_ref[0, s:s + 1, 0:MLA_ROPE]
            sc = (jnp.sum(qlf * c, axis=-1, keepdims=True) + jnp.sum(qrf * r, axis=-1, keepdims=True)) * MLA_SCALE
            ss.append(jnp.where(t_row >= s, sc, -jnp.inf))
            cs.append(c)
        m = ss[0]
        for s in range(1, DEC_SEQ):
            m = jnp.maximum(m, ss[s])
        l = jnp.zeros_like(m)
        acc = jnp.zeros(acc_sc.shape, F32)
        for s in range(DEC_SEQ):
            p = jnp.exp(ss[s] - m)
            l = l + p
            acc = acc + p * cs[s]
        m_sc[...] = jnp.broadcast_to(m, m_sc.shape)
        l_sc[...] = jnp.broadcast_to(l, l_sc.shape)
        acc_sc[...] = acc

    for cp in group_copies(b, g, slot):
        cp.wait()

    keys = MLA_SUB * PAGE
    m_new = m_sc[...]
    l_new = l_sc[...]
    acc_new = acc_sc[...]
    for blk in range(MLA_GROUP // MLA_SUB):
        cb = cbuf[slot, blk * keys:(blk + 1) * keys, :].astype(BF16)
        s_rope = jnp.concatenate(
            [_dot(qr, rbuf[slot, blk * MLA_SUB + p].astype(BF16)) for p in range(MLA_SUB)], axis=1)
        s = (_dot_nt(ql, cb) + s_rope) * MLA_SCALE
        m_prev = m_new
        m_new = jnp.maximum(m_prev, jnp.max(s, axis=-1, keepdims=True))
        alpha = jnp.exp(m_prev - m_new)
        pr = jnp.exp(s - _tile_lanes(m_new, keys // 128))
        l_new = alpha * l_new + jnp.sum(pr, axis=-1, keepdims=True)
        acc_new = _tile_lanes(alpha, 2) * acc_new + _dot(pr.astype(BF16), cb)
    m_sc[...] = m_new
    l_sc[...] = l_new
    acc_sc[...] = acc_new

    @pl.when(g == ng - 1)
    def _():
        o_ref[0] = (acc_new / _tile_lanes(l_new, 2)).astype(o_ref.dtype)


def mla_sample(pt_flat, q_lat3, q_full3, c_new, r_new, cache_ckv, cache_krT, layer):
    ng = N_PAGES // MLA_GROUP
    rows = DEC_SEQ * MLA_HEADS
    kern = functools.partial(_mla_sample_kernel, layer=layer)
    per_b = lambda b, g, pt: (b, 0, 0)
    return pl.pallas_call(
        kern,
        grid_spec=pltpu.PrefetchScalarGridSpec(
            num_scalar_prefetch=1,
            grid=(DEC_BATCH, ng),
            in_specs=[
                pl.BlockSpec((1, rows, 256), per_b),
                pl.BlockSpec((1, rows, 256), per_b),
                pl.BlockSpec((1, DEC_SEQ, 256), per_b),
                pl.BlockSpec((1, DEC_SEQ, 128), per_b),
                pl.BlockSpec(memory_space=pl.ANY),
                pl.BlockSpec(memory_space=pl.ANY),
            ],
            out_specs=pl.BlockSpec((1, rows, 256), per_b),
            scratch_shapes=[
                pltpu.VMEM((2, MLA_GROUP * PAGE, MLA_KV_LORA), F32),
                pltpu.VMEM((2, MLA_GROUP, MLA_ROPE, PAGE), F32),
                pltpu.SemaphoreType.DMA((2,)),
                pltpu.SemaphoreType.DMA((2,)),
                pltpu.VMEM((rows, 128), F32), pltpu.VMEM((rows, 128), F32), pltpu.VMEM((rows, 256), F32),
            ],
        ),
        out_shape=jax.ShapeDtypeStruct((DEC_BATCH, rows, 256), BF16),
        compiler_params=_cp(("arbitrary", "arbitrary")),
        name="mla_sample",
    )(pt_flat, q_lat3, q_full3, c_new, r_new, cache_ckv, cache_krT)


def _sb_sample_kernel(pt_ref, q_ref, kn_ref, vn_ref, later_ref, ck_hbm, cv_hbm, o_ref,
                      kbuf, vbuf, ksem, vsem, *, layer):
    b = pl.program_id(0)
    slot = lax.rem(b, 2)
    rows = DEC_SEQ * SB_HEADS
    n_groups = N_PAGES // SB_GROUP

    def group_copies(bb, gg, sl):
        cps = []
        for p in range(SB_GROUP):
            page = pt_ref[bb * N_PAGES + (N_PAGES - 1) - (gg * SB_GROUP + p)]
            dst = pl.ds((SB_GROUP - 1 - p) * PAGE, PAGE)
            cps.append(pltpu.make_async_copy(ck_hbm.at[layer, page], kbuf.at[sl, dst], ksem.at[sl]))
            cps.append(pltpu.make_async_copy(cv_hbm.at[layer, page], vbuf.at[sl, dst], vsem.at[sl]))
        return cps

    @pl.when(b == 0)
    def _():
        for cp in group_copies(0, 0, 0):
            cp.start()

    @pl.when(b + 1 < pl.num_programs(0))
    def _():
        for cp in group_copies(b + 1, 0, 1 - slot):
            cp.start()

    qf = q_ref[0]
    qb = qf.astype(BF16)

    t_row = lax.broadcasted_iota(jnp.int32, (rows, 1), 0) // SB_HEADS
    ls, neg, valid = [], [], []
    for s in range(DEC_SEQ):
        z = jnp.sum(qf * kn_ref[0, s:s + 1, :], axis=-1, keepdims=True) * SB_SCALE
        t = _softplus_neg_abs(z)
        ok = t_row > s
        valid.append(ok)
        ls.append(jnp.minimum(z, 0.0) - t)
        neg.append(jnp.where(ok, jnp.minimum(-z, 0.0) - t, 0.0))
    acc = jnp.zeros((rows, SB_DIM), F32)
    carry = jnp.zeros((rows, 1), F32)
    for s in range(DEC_SEQ - 1, -1, -1):
        a = jnp.where(valid[s], jnp.exp(ls[s] + carry), 0.0)
        acc = acc + a * vn_ref[0, s:s + 1, :]
        carry = carry + neg[s]

    def process(sl, carry, acc):
        k = kbuf[sl].astype(BF16)
        v = vbuf[sl].astype(BF16)
        z = _dot_nt(qb, k) * SB_SCALE
        t = _softplus_neg_abs(z)
        ls = jnp.minimum(z, 0.0) - t
        neg = jnp.minimum(-z, 0.0) - t
        hi, lo = _split2(neg)
        later = later_ref[...]
        after = _dot(hi, later) + _dot(lo, later) + carry
        a = jnp.exp(ls + after)
        return carry + jnp.sum(neg, axis=-1, keepdims=True), acc + _dot(a.astype(BF16), v)

    for cp in group_copies(b, 0, slot):
        cp.wait()
    carry, acc = process(slot, carry, acc)

    def more(st):
        return jnp.logical_and(st[0] < n_groups, jnp.max(st[1]) >= SB_DONE)

    def fetch_and_process(st):
        gg, carry, acc = st
        cps = group_copies(b, gg, 2)
        for cp in cps:
            cp.start()
        for cp in cps:
            cp.wait()
        carry, acc = process(2, carry, acc)
        return gg + 1, carry, acc

    _, _, acc = lax.while_loop(more, fetch_and_process, (jnp.int32(1), carry, acc))
    o_ref[0] = acc.astype(o_ref.dtype)


def sb_sample(pt_flat, q3, k_new, v_new, cache_k, cache_v, layer):
    rows = DEC_SEQ * SB_HEADS
    keys = SB_GROUP * PAGE
    kern = functools.partial(_sb_sample_kernel, layer=layer)
    later = (jnp.arange(keys)[:, None] > jnp.arange(keys)[None, :]).astype(BF16)
    per_b = lambda b, pt: (b, 0, 0)
    return pl.pallas_call(
        kern,
        grid_spec=pltpu.PrefetchScalarGridSpec(
            num_scalar_prefetch=1,
            grid=(DEC_BATCH,),
            in_specs=[
                pl.BlockSpec((1, rows, SB_DIM), per_b),
                pl.BlockSpec((1, DEC_SEQ, SB_DIM), per_b),
                pl.BlockSpec((1, DEC_SEQ, SB_DIM), per_b),
                pl.BlockSpec((keys, keys), lambda b, pt: (0, 0)),
                pl.BlockSpec(memory_space=pl.ANY),
                pl.BlockSpec(memory_space=pl.ANY),
            ],
            out_specs=pl.BlockSpec((1, rows, SB_DIM), per_b),
            scratch_shapes=[
                pltpu.VMEM((3, keys, SB_DIM), F32),
                pltpu.VMEM((3, keys, SB_DIM), F32),
                pltpu.SemaphoreType.DMA((3,)),
                pltpu.SemaphoreType.DMA((3,)),
            ],
        ),
        out_shape=jax.ShapeDtypeStruct((DEC_BATCH, rows, SB_DIM), BF16),
        compiler_params=_cp(("arbitrary",)),
        name="sb_sample",
    )(pt_flat, q3, k_new, v_new, later, cache_k, cache_v)


def _merge_kernel(mla_p_ref, sb_p_ref, hgo_p_ref, mla_s_ref, sb_s_ref, hgo_s_ref, hgg_ref, gm_ref, gs_ref, gh_ref,
                  hn_ref, lm_ref, ls_ref, lh_ref, o_ref, mla_sc, sb_sc, hg_sc):
    is_sample = pl.program_id(0) == pl.num_programs(0) - 1
    first = pl.program_id(1) == 0

    def stage(mla_ref, sb_ref, hgo_ref):
        mla_sc[...] = mla_ref[...]
        sb_sc[...] = sb_ref[...]
        gate = hgg_ref[...]
        act = gate * _sigmoid(gate)
        for h in range(HG_HEADS):
            cs = slice(h * HG_DIM, (h + 1) * HG_DIM)
            hg_sc[:, cs] = (_rms(hgo_ref[:, cs], hn_ref[...]) * act[:, cs]).astype(BF16)

    @pl.when(jnp.logical_and(first, jnp.logical_not(is_sample)))
    def _():
        stage(mla_p_ref, sb_p_ref, hgo_p_ref)

    @pl.when(jnp.logical_and(first, is_sample))
    def _():
        stage(mla_s_ref, sb_s_ref, hgo_s_ref)

    bn = o_ref.shape[1]
    cols = pl.ds(pl.multiple_of(pl.program_id(1) * bn, bn), bn)
    m = (_sigmoid(gm_ref[...]) * _dot(mla_sc[...], lm_ref[:, cols])
         + _sigmoid(gs_ref[...]) * _dot(sb_sc[...], ls_ref[:, cols])
         + _sigmoid(gh_ref[...]) * _dot(hg_sc[...], lh_ref[:, cols]))
    o_ref[...] = m.astype(o_ref.dtype)


def merge(mla_p, sb_p, hg_p, mla_s, sb_s, hg_s, y, hg_norm, lift_mla, lift_sb, lift_hg, layer, bn=512):
    bm = S_ROWS
    n_prompt = P_ROWS // bm
    prow = lambda i, j: (jnp.minimum(i, n_prompt - 1), 0)
    srow = lambda i, j: (0, 0)
    row = lambda c: (lambda i, j: (i, c))
    gate = lambda c0: (lambda i, j: (i, c0 // bn + j))
    wfull = lambda i, j: (layer, 0, 0)
    once = pl.Buffered(1)
    return pl.pallas_call(
        _merge_kernel,
        grid=(M_ROWS // bm, D_MODEL // bn),
        in_specs=[
            pl.BlockSpec((bm, 1024), prow),
            pl.BlockSpec((bm, 512), prow),
            pl.BlockSpec((bm, 512), prow),
            pl.BlockSpec((bm, 1024), srow),
            pl.BlockSpec((bm, 512), srow),
            pl.BlockSpec((bm, 512), srow),
            pl.BlockSpec((bm, 512), row(C_HGG // 512)),
            pl.BlockSpec((bm, bn), gate(C_GMLA)),
            pl.BlockSpec((bm, bn), gate(C_GSB)),
            pl.BlockSpec((bm, bn), gate(C_GHG)),
            pl.BlockSpec((1, HG_DIM), lambda i, j: (0, 0)),
            pl.BlockSpec((None, 1024, D_MODEL), wfull, pipeline_mode=once),
            pl.BlockSpec((None, 512, D_MODEL), wfull, pipeline_mode=once),
            pl.BlockSpec((None, 512, D_MODEL), wfull, pipeline_mode=once),
        ],
        out_specs=pl.BlockSpec((bm, bn), lambda i, j: (i, j)),
        out_shape=jax.ShapeDtypeStruct((M_ROWS, D_MODEL), BF16),
        scratch_shapes=[pltpu.VMEM((bm, 1024), BF16), pltpu.VMEM((bm, 512), BF16), pltpu.VMEM((bm, 512), BF16)],
        compiler_params=_cp(("parallel", "arbitrary")),
        name="merge",
    )(mla_p, sb_p, hg_p, mla_s, sb_s, hg_s, y, y, y, y, hg_norm.reshape(1, -1), lift_mla, lift_sb, lift_hg)


def _ffn_gu_kernel(x_ref, gn_ref, wg_ref, wu_ref, o_ref, h_sc):
    @pl.when(pl.program_id(1) == 0)
    def _():
        h_sc[...] = _rms(x_ref[...], gn_ref[...]).astype(BF16)

    h = h_sc[...]
    g = _dot(h, wg_ref[...])
    u = _dot(h, wu_ref[...])
    o_ref[...] = (g * _sigmoid(g) * u).astype(o_ref.dtype)


def ffn_gate_up(x, gn, wg, wu, layer, bm=BM, bn=512):
    return pl.pallas_call(
        _ffn_gu_kernel,
        grid=(M_ROWS // bm, D_FF // bn),
        in_specs=[pl.BlockSpec((bm, D_MODEL), lambda i, j: (i, 0)),
                  pl.BlockSpec((1, D_MODEL), lambda i, j: (0, 0)),
                  pl.BlockSpec((None, D_MODEL, bn), lambda i, j: (layer, 0, j)),
                  pl.BlockSpec((None, D_MODEL, bn), lambda i, j: (layer, 0, j))],
        out_specs=pl.BlockSpec((bm, bn), lambda i, j: (i, j)),
        out_shape=jax.ShapeDtypeStruct((M_ROWS, D_FF), BF16),
        scratch_shapes=[pltpu.VMEM((bm, D_MODEL), BF16)],
        compiler_params=_cp(("parallel", "arbitrary")),
        name="ffn_gate_up",
    )(x, gn.reshape(1, -1), wg, wu)


def _rope_tables():
    half = MLA_ROPE // 2
    inv = ROPE_THETA ** (-jnp.arange(half, dtype=F32) / half)
    pos = jnp.concatenate([
        jnp.tile(jnp.arange(SEQ, dtype=jnp.int32), N_BATCH),
        N_PAGES * PAGE + jnp.tile(jnp.arange(DEC_SEQ, dtype=jnp.int32), DEC_BATCH),
    ]).astype(F32)
    ang = pos[:, None] * inv[None, :]
    pad = jnp.zeros((M_ROWS, 128 - MLA_ROPE), F32)
    cos = jnp.concatenate([jnp.cos(ang), jnp.cos(ang), pad], axis=-1)
    sin = jnp.concatenate([jnp.sin(ang), jnp.sin(ang), pad], axis=-1)
    return cos, sin


def _permute_w_in(w):
    pad = jnp.zeros((w.shape[0], N_IN - 9792), w.dtype)
    return jnp.concatenate([w[:, 0:512], w[:, 832:1344], w[:, 512:768], w[:, 1344:9792], w[:, 768:832], pad],
                           axis=-1).astype(BF16)


def kernel(x_prompt, x_sample, cache_mla_ckv, cache_mla_krope, cache_sb_k, cache_sb_v, state_hgrn, page_table, norm_attn, w_in, mla_q_norm, mla_w_uq, mla_kv_norm, mla_w_uk, mla_w_uv, hg_lb_logits, hg_norm, w_lift_mla, w_lift_sb, w_lift_hg, w_out, norm_ffn, ffn_w_gate, ffn_w_up, ffn_w_down, norm_final):
    cos, sin = _rope_tables()
    lb_all = lower_bounds(hg_lb_logits.astype(F32))
    pt_flat = page_table.reshape(-1).astype(jnp.int32)
    cache_krT = jnp.swapaxes(cache_mla_krope, 2, 3)
    lift_mla_b, lift_sb_b, lift_hg_b = (w.astype(BF16) for w in (w_lift_mla, w_lift_sb, w_lift_hg))
    w_out_b, w_gate_b, w_up_b, w_down_b = (w.astype(BF16) for w in (w_out, ffn_w_gate, ffn_w_up, ffn_w_down))
    x = jnp.concatenate([x_prompt.reshape(P_ROWS, D_MODEL), x_sample.reshape(S_ROWS, D_MODEL)], axis=0)

    outs = {k: [] for k in ("p_ckv", "p_kr", "p_sk", "p_sv", "p_hs", "s_ckv", "s_kr", "s_sk", "s_sv", "s_hs")}
    for l in range(DEPTH):
        w_in_p = _permute_w_in(w_in[l])
        w_uq = jnp.pad(mla_w_uq[l].reshape(MLA_Q_LORA, MLA_HEADS, MLA_NOPE + MLA_ROPE),
                       ((0, 0), (0, 0), (0, MLA_QK - MLA_NOPE - MLA_ROPE))).reshape(MLA_Q_LORA, -1).astype(BF16)
        w_uk2 = mla_w_uk[l].reshape(MLA_KV_LORA, -1).astype(BF16)
        w_uv2 = mla_w_uv[l].reshape(MLA_KV_LORA, -1).astype(BF16)
        w_ukT = jnp.transpose(mla_w_uk[l], (1, 2, 0)).astype(BF16)
        w_uv3 = jnp.transpose(mla_w_uv[l], (1, 0, 2)).astype(BF16)

        y = norm_matmul(x, norm_attn[l], w_in_p, 768, F32, name="in_proj")
        q_full, ckv, kr, logf, kf = prep(y, cos, sin, mla_q_norm[l], w_uq, mla_kv_norm[l], lb_all[l])

        k_full, v = kv_up(ckv, kr, w_uk2, w_uv2)
        mla_p = mla_prompt(q_full, k_full, v)
        sb_p = sb_prompt(y)
        hg_p, hs_p = hgrn_prompt(y, logf, kf)

        q_lat = q_latent(q_full, w_ukT)
        o_lat = mla_sample(
            pt_flat,
            q_lat.reshape(DEC_BATCH, DEC_SEQ * MLA_HEADS, 256),
            q_full[P_ROWS:].reshape(DEC_BATCH, DEC_SEQ * MLA_HEADS, MLA_QK),
            ckv[P_ROWS:].reshape(DEC_BATCH, DEC_SEQ, MLA_KV_LORA),
            kr[P_ROWS:].reshape(DEC_BATCH, DEC_SEQ, 128),
            cache_mla_ckv, cache_krT, l)
        mla_s = o_latent_up(o_lat.reshape(S_ROWS, MLA_HEADS * 256), w_uv3)
        sb_s = sb_sample(
            pt_flat,
            y[P_ROWS:, C_SBQ:C_SBQ + 512].reshape(DEC_BATCH, DEC_SEQ * SB_HEADS, SB_DIM),
            y[P_ROWS:, C_SBK:C_SBK + 128].reshape(DEC_BATCH, DEC_SEQ, SB_DIM),
            y[P_ROWS:, C_SBV:C_SBV + 128].reshape(DEC_BATCH, DEC_SEQ, SB_DIM),
            cache_sb_k, cache_sb_v, l)
        hg_s, hs_s = hgrn_sample(y, logf, kf, state_hgrn, l)

        m = merge(mla_p, sb_p, hg_p, mla_s, sb_s.reshape(S_ROWS, SB_HEADS * SB_DIM), hg_s, y, hg_norm[l],
                  lift_mla_b, lift_sb_b, lift_hg_b, l)
        x = matmul(m, w_out_b, l, 512, F32, res=x, name="out_proj")
        act = ffn_gate_up(x, norm_ffn[l], w_gate_b, w_up_b, l)
        x = matmul(act, w_down_b, l, 512, F32, res=x, vmem=VMEM_LIMIT_BIG, name="ffn_down")

        outs["p_ckv"].append(ckv[:P_ROWS].reshape(N_BATCH, SEQ, MLA_KV_LORA))
        outs["p_kr"].append(kr[:P_ROWS, :MLA_ROPE].reshape(N_BATCH, SEQ, MLA_ROPE))
        outs["p_sk"].append(y[:P_ROWS, C_SBK:C_SBK + 128].reshape(N_BATCH, SEQ, SB_DIM))
        outs["p_sv"].append(y[:P_ROWS, C_SBV:C_SBV + 128].reshape(N_BATCH, SEQ, SB_DIM))
        outs["p_hs"].append(hs_p.reshape(N_BATCH, HG_HEADS, HG_DIM, HG_DIM))
        outs["s_ckv"].append(ckv[P_ROWS:].reshape(DEC_BATCH, DEC_SEQ, MLA_KV_LORA))
        outs["s_kr"].append(kr[P_ROWS:, :MLA_ROPE].reshape(DEC_BATCH, DEC_SEQ, MLA_ROPE))
        outs["s_sk"].append(y[P_ROWS:, C_SBK:C_SBK + 128].reshape(DEC_BATCH, DEC_SEQ, SB_DIM))
        outs["s_sv"].append(y[P_ROWS:, C_SBV:C_SBV + 128].reshape(DEC_BATCH, DEC_SEQ, SB_DIM))
        outs["s_hs"].append(hs_s)

    yf = rmsnorm(x, norm_final, F32)
    y_prompt = yf[:P_ROWS].reshape(N_BATCH, SEQ, D_MODEL)
    y_sample = yf[P_ROWS:].reshape(DEC_BATCH, DEC_SEQ, D_MODEL)
    st = lambda k: jnp.stack(outs[k])
    return (y_prompt, y_sample, st("p_ckv"), st("p_kr"), st("p_sk"), st("p_sv"), st("p_hs"),
            st("s_ckv"), st("s_kr"), st("s_sk"), st("s_sv"), st("s_hs"))
```
